```python
import math
import jax
import jax.numpy as jnp
from jax import lax
import numpy as np

D_MODEL = 1024
BATCH = 2
SEQ = 8192
DEPTH = 4

N_MIXERS = 3
Q_BLOCK = 128
SB_HEADS = 16
SB_HEAD_DIM = D_MODEL // SB_HEADS
SB_SCALE = 1.0 / math.sqrt(SB_HEAD_DIM)
SC_WIDTH = 3
MLA_HEADS = 8
QK_NOPE = 128
QK_ROPE = 64
V_HEAD = 128
Q_LORA = 384
KV_LORA = 256
MLA_SCALE = 1.0 / math.sqrt(QK_NOPE + QK_ROPE)
ROPE_THETA = 10000.0
N_EXPERTS = 32
TOP_K = 4
D_EXPERT = D_MODEL
SWIGLU_LIMIT = 7.0
SWIGLU_ALPHA = 1.702
MOE_BLOCK = 128
LN_EPS = 1e-5
RMS_EPS = 1e-6
DEEPNORM_ALPHA = (2 * DEPTH) ** 0.25
DEEPNORM_BETA = (8 * DEPTH) ** -0.25
N_SB_LAYERS = (DEPTH - 0 + N_MIXERS - 1) // N_MIXERS
N_SC_LAYERS = (DEPTH - 1 + N_MIXERS - 1) // N_MIXERS
N_MLA_LAYERS = (DEPTH - 2 + N_MIXERS - 1) // N_MIXERS

kernel_name = "hybrid_sb_conv_mla_moe_deepnorm"


def layer_norm(x, g, b):
    xf = x.astype(jnp.float32)
    mu = jnp.mean(xf, axis=-1, keepdims=True)
    var = jnp.mean(jnp.square(xf - mu), axis=-1, keepdims=True)
    return ((xf - mu) * lax.rsqrt(var + LN_EPS)).astype(x.dtype) * g + b


def rms_norm(x, g):
    xf = x.astype(jnp.float32)
    ms = jnp.mean(jnp.square(xf), axis=-1, keepdims=True)
    return (xf * lax.rsqrt(ms + RMS_EPS)).astype(x.dtype) * g


def stick_breaking_attention(x, w_in, w_out):
    b, s, _ = x.shape
    nb = s // Q_BLOCK
    q, k, v = jnp.split(x @ w_in, 3, axis=-1)

    def to_blocks(t):
        return t.reshape(b, nb, Q_BLOCK, SB_HEADS, SB_HEAD_DIM).transpose(0, 3, 1, 2, 4)

    q = to_blocks(q) * SB_SCALE
    k = to_blocks(k)
    v = to_blocks(v)
    q_blocks = q.transpose(2, 0, 1, 3, 4)
    idx = jnp.arange(Q_BLOCK)
    strict_diag = idx[:, None] > idx[None, :]

    def query_block(args):
        qb, qblk = args

        def step(i, carry):
            acc, log_surv = carry
            kb = qb - i
            kblk = lax.dynamic_index_in_dim(k, kb, axis=2, keepdims=False)
            vblk = lax.dynamic_index_in_dim(v, kb, axis=2, keepdims=False)
            z = jnp.einsum("bhqd,bhkd->bhqk", qblk, kblk).astype(jnp.float32)
            valid = jnp.where(kb == qb, strict_diag, True)
            log_1m = jnp.where(valid, jax.nn.log_sigmoid(-z), 0.0)
            later = lax.cumsum(log_1m, axis=3, reverse=True) - log_1m
            log_a = jax.nn.log_sigmoid(z) + later + log_surv[..., None]
            a = jnp.where(valid, jnp.exp(log_a), 0.0)
            acc = acc + jnp.einsum("bhqk,bhkd->bhqd", a, vblk.astype(jnp.float32))
            return acc, log_surv + jnp.sum(log_1m, axis=-1)

        init = (jnp.zeros((b, SB_HEADS, Q_BLOCK, SB_HEAD_DIM), jnp.float32),
                jnp.zeros((b, SB_HEADS, Q_BLOCK), jnp.float32))
        acc, _ = lax.fori_loop(0, qb + 1, step, init)
        return acc

    out = lax.map(query_block, (jnp.arange(nb), q_blocks))
    out = out.transpose(1, 0, 3, 2, 4).reshape(b, s, D_MODEL).astype(x.dtype)
    return out @ w_out


def short_conv_mixer(x, w_in, conv_w, w_out):
    s = x.shape[1]
    gate_b, gate_c, h = jnp.split(x @ w_in, 3, axis=-1)
    u = gate_c * h
    u_pad = jnp.pad(u, ((0, 0), (SC_WIDTH - 1, 0), (0, 0)))
    conv = conv_w[0] * u_pad[:, 0:s]
    for j in range(1, SC_WIDTH):
        conv = conv + conv_w[j] * u_pad[:, j:j + s]
    return (gate_b * conv) @ w_out


def apply_rope(t, cos, sin):
    t1, t2 = jnp.split(t, 2, axis=-1)
    return jnp.concatenate([t1 * cos - t2 * sin, t2 * cos + t1 * sin], axis=-1)


def multi_head_latent_attention(x, positions, w_in, q_norm, w_uq, kv_norm, w_ukv, w_out):
    b, s, _ = x.shape
    nb = s // Q_BLOCK
    q_lat, kv_lat, k_rope = jnp.split(x @ w_in, [Q_LORA, Q_LORA + KV_LORA], axis=-1)
    q = (rms_norm(q_lat, q_norm) @ w_uq).reshape(b, s, MLA_HEADS, QK_NOPE + QK_ROPE)
    kv = (rms_norm(kv_lat, kv_norm) @ w_ukv).reshape(b, s, MLA_HEADS, QK_NOPE + V_HEAD)
    q_nope, q_rope = jnp.split(q, [QK_NOPE], axis=-1)
    k_nope, v = jnp.split(kv, [QK_NOPE], axis=-1)
    inv_freq = ROPE_THETA ** (-jnp.arange(0, QK_ROPE, 2, dtype=jnp.float32) / QK_ROPE)
    ang = positions.astype(jnp.float32)[..., None] * inv_freq
    cos, sin = jnp.cos(ang), jnp.sin(ang)
    q_rope = apply_rope(q_rope, cos[:, :, None], sin[:, :, None]).astype(x.dtype)
    k_rope = apply_rope(k_rope, cos, sin).astype(x.dtype)

    def to_blocks(t):
        return t.reshape(b, nb, Q_BLOCK, MLA_HEADS, t.shape[-1]).transpose(1, 0, 2, 3, 4)

    key_pos = jnp.arange(s)

    def query_block(args):
        qb, qn, qr = args
        scores = (jnp.einsum("bqhn,bkhn->bhqk", qn, k_nope)
                  + jnp.einsum("bqhr,bkr->bhqk", qr, k_rope)).astype(jnp.float32) * MLA_SCALE
        q_pos = qb * Q_BLOCK + jnp.arange(Q_BLOCK)
        scores = jnp.where(key_pos[None, :] <= q_pos[:, None], scores, -jnp.inf)
        p = jax.nn.softmax(scores, axis=-1)
        return jnp.einsum("bhqk,bkhv->bqhv", p.astype(v.dtype), v)

    out = lax.map(query_block, (jnp.arange(nb), to_blocks(q_nope), to_blocks(q_rope)))
    out = out.transpose(1, 0, 2, 3, 4).reshape(b, s, MLA_HEADS * V_HEAD)
    return out @ w_out


def routed_expert_ffn(x, router_w, router_b, w_gate, b_gate, w_up, b_up, w_down, b_down):
    b, s, d = x.shape
    n = b * s
    xt = x.reshape(n, d)
    logits = (xt @ router_w + router_b).astype(jnp.float32)
    top_logits, top_idx = lax.top_k(logits, TOP_K)
    gates = jax.nn.softmax(top_logits, axis=-1)
    flat_e = top_idx.reshape(-1)
    flat_tok = jnp.repeat(jnp.arange(n, dtype=jnp.int32), TOP_K)
    flat_g = gates.reshape(-1)
    order = jnp.argsort(flat_e)
    sorted_e = flat_e[order]
    counts = jnp.bincount(flat_e, length=N_EXPERTS)
    padded = (counts + MOE_BLOCK - 1) // MOE_BLOCK * MOE_BLOCK
    group_start = jnp.cumsum(counts) - counts
    padded_end = jnp.cumsum(padded)
    padded_start = padded_end - padded
    dest = padded_start[sorted_e] + (jnp.arange(n * TOP_K) - group_start[sorted_e])
    n_rows = n * TOP_K + N_EXPERTS * MOE_BLOCK
    n_blocks = n_rows // MOE_BLOCK
    row_tok = jnp.full((n_rows,), n, jnp.int32).at[dest].set(flat_tok[order])
    row_gate = jnp.zeros((n_rows,), jnp.float32).at[dest].set(flat_g[order])
    block_expert = jnp.minimum(
        jnp.searchsorted(padded_end, jnp.arange(n_blocks) * MOE_BLOCK, side="right"), N_EXPERTS - 1)
    x_pad = jnp.concatenate([xt, jnp.zeros((1, d), xt.dtype)], axis=0)
    rows = x_pad[row_tok].reshape(n_blocks, MOE_BLOCK, d)

    def expert_block(args):
        e, xb = args
        g = jnp.minimum(xb @ w_gate[e] + b_gate[e], SWIGLU_LIMIT)
        u = jnp.clip(xb @ w_up[e] + b_up[e], -SWIGLU_LIMIT, SWIGLU_LIMIT)
        hdn = g * jax.nn.sigmoid(SWIGLU_ALPHA * g) * (u + 1.0)
        return hdn @ w_down[e] + b_down[e]

    y = lax.map(expert_block, (block_expert, rows)).reshape(n_rows, d)
    y = y * row_gate[:, None].astype(y.dtype)
    out = jax.ops.segment_sum(y, row_tok, num_segments=n + 1)[:n]
    return out.reshape(b, s, d)


def _normal(key, shape, scale):
    return scale * jax.random.normal(key, shape, jnp.float32)


def setup_inputs(seed: int = 0) -> dict:
    key = jax.random.key(seed)
    ks = jax.random.split(key, 24)
    D = D_MODEL
    r_mla = Q_LORA + KV_LORA + QK_ROPE
    return {
        "x": _normal(ks[0], (BATCH, SEQ, D), 1.0),
        "positions": jnp.arange(SEQ, dtype=jnp.int32)[None, :]
        + jax.random.randint(ks[1], (BATCH, 1), 0, 1024, dtype=jnp.int32),
        "ln_g": 1.0 + _normal(ks[2], (DEPTH, 2, D), 0.02),
        "ln_b": _normal(ks[3], (DEPTH, 2, D), 0.02),
        "sb_w_in": _normal(ks[4], (N_SB_LAYERS, D, 3 * D), D ** -0.5),
        "sb_w_out": _normal(ks[5], (N_SB_LAYERS, D, D), DEEPNORM_BETA * D ** -0.5),
        "sc_w_in": _normal(ks[6], (N_SC_LAYERS, D, 3 * D), D ** -0.5),
        "sc_conv_w": _normal(ks[7], (N_SC_LAYERS, SC_WIDTH, D), SC_WIDTH ** -0.5),
        "sc_w_out": _normal(ks[8], (N_SC_LAYERS, D, D), DEEPNORM_BETA * D ** -0.5),
        "mla_w_in": _normal(ks[9], (N_MLA_LAYERS, D, r_mla), D ** -0.5),
        "mla_q_norm": 1.0 + _normal(ks[10], (N_MLA_LAYERS, Q_LORA), 0.02),
        "mla_w_uq": _normal(ks[11], (N_MLA_LAYERS, Q_LORA, MLA_HEADS * (QK_NOPE + QK_ROPE)), Q_LORA ** -0.5),
        "mla_kv_norm": 1.0 + _normal(ks[12], (N_MLA_LAYERS, KV_LORA), 0.02),
        "mla_w_ukv": _normal(ks[13], (N_MLA_LAYERS, KV_LORA, MLA_HEADS * (QK_NOPE + V_HEAD)), KV_LORA ** -0.5),
        "mla_w_out": _normal(ks[14], (N_MLA_LAYERS, MLA_HEADS * V_HEAD, D),
                              DEEPNORM_BETA * (MLA_HEADS * V_HEAD) ** -0.5),
        "router_w": _normal(ks[15], (DEPTH, D, N_EXPERTS), D ** -0.5),
        "router_b": _normal(ks[16], (DEPTH, N_EXPERTS), 0.01),
        "moe_w_gate": _normal(ks[17], (DEPTH, N_EXPERTS, D, D_EXPERT), D ** -0.5),
        "moe_b_gate": _normal(ks[18], (DEPTH, N_EXPERTS, D_EXPERT), 0.02),
        "moe_w_up": _normal(ks[19], (DEPTH, N_EXPERTS, D, D_EXPERT), D ** -0.5),
        "moe_b_up": _normal(ks[20], (DEPTH, N_EXPERTS, D_EXPERT), 0.02),
        "moe_w_down": _normal(ks[21], (DEPTH, N_EXPERTS, D_EXPERT, D), DEEPNORM_BETA * D_EXPERT ** -0.5),
        "moe_b_down": _normal(ks[22], (DEPTH, N_EXPERTS, D), 0.02),
    }


def reference(x, positions, ln_g, ln_b, sb_w_in, sb_w_out, sc_w_in, sc_conv_w, sc_w_out,
              mla_w_in, mla_q_norm, mla_w_uq, mla_kv_norm, mla_w_ukv, mla_w_out,
              router_w, router_b, moe_w_gate, moe_b_gate, moe_w_up, moe_b_up,
              moe_w_down, moe_b_down):
    h = x
    for layer in range(DEPTH):
        kind, slot = layer % N_MIXERS, layer // N_MIXERS
        if kind == 0:
            mix = stick_breaking_attention(h, sb_w_in[slot], sb_w_out[slot])
        elif kind == 1:
            mix = short_conv_mixer(h, sc_w_in[slot], sc_conv_w[slot], sc_w_out[slot])
        else:
            mix = multi_head_latent_attention(h, positions, mla_w_in[slot], mla_q_norm[slot],
                                              mla_w_uq[slot], mla_kv_norm[slot],
                                              mla_w_ukv[slot], mla_w_out[slot])
        h = layer_norm(DEEPNORM_ALPHA * h + mix, ln_g[layer, 0], ln_b[layer, 0])
        ffn = routed_expert_ffn(h, router_w[layer], router_b[layer], moe_w_gate[layer],
                                moe_b_gate[layer], moe_w_up[layer], moe_b_up[layer],
                                moe_w_down[layer], moe_b_down[layer])
        h = layer_norm(DEEPNORM_ALPHA * h + ffn, ln_g[layer, 1], ln_b[layer, 1])
    return h
```

```python
import functools
import math

import jax
import jax.numpy as jnp
from jax import lax
from jax.experimental import pallas as pl
from jax.experimental.pallas import tpu as pltpu

F32 = jnp.float32
BF16 = jnp.bfloat16

D_MODEL = 1024
DEPTH = 4
N_MIXERS = 3
SB_HEADS = 16
SB_HEAD_DIM = D_MODEL // SB_HEADS
SB_SCALE = 1.0 / math.sqrt(SB_HEAD_DIM)
SC_WIDTH = 3
MLA_HEADS = 8
QK_NOPE = 128
QK_ROPE = 64
V_HEAD = 128
Q_LORA = 384
KV_LORA = 256
MLA_SCALE = 1.0 / math.sqrt(QK_NOPE + QK_ROPE)
ROPE_THETA = 10000.0
N_EXPERTS = 32
TOP_K = 4
SWIGLU_LIMIT = 7.0
SWIGLU_ALPHA = 1.702
LN_EPS = 1e-5
RMS_EPS = 1e-6
DEEPNORM_ALPHA = (2 * DEPTH) ** 0.25

LANES = 128
SUBLANES = 8
VMEM_LIMIT_BYTES = 56 * 1024 * 1024

ROW_TILE = 512
SB_TILE = 128
MLA_TILE = 256
MOE_ROWS = 256
COMBINE_ROWS = 128
MLA_QK_PAD = 256


def _params(*semantics):
    return pltpu.CompilerParams(dimension_semantics=semantics,
                                vmem_limit_bytes=VMEM_LIMIT_BYTES)


def _deepnorm_layer_norm(h, sub, g, b):
    y = DEEPNORM_ALPHA * h + sub
    mu = jnp.mean(y, axis=-1, keepdims=True)
    d = y - mu
    var = jnp.mean(d * d, axis=-1, keepdims=True)
    return d * lax.rsqrt(var + LN_EPS) * g + b


def _proj_kernel(x_ref, w_ref, o_ref):
    o_ref[...] = jnp.dot(x_ref[...], w_ref[...],
                         preferred_element_type=F32).astype(o_ref.dtype)


def _proj(x16, w16, out_dtype, tn):
    n, k = x16.shape
    nout = w16.shape[1]
    tm = min(ROW_TILE, n)
    return pl.pallas_call(
        _proj_kernel,
        grid=(nout // tn, n // tm),
        in_specs=[pl.BlockSpec((tm, k), lambda j, i: (i, 0)),
                  pl.BlockSpec((k, tn), lambda j, i: (0, j))],
        out_specs=pl.BlockSpec((tm, tn), lambda j, i: (i, j)),
        out_shape=jax.ShapeDtypeStruct((n, nout), out_dtype),
        compiler_params=_params("arbitrary", "arbitrary"),
        name="proj",
    )(x16, w16)


def _outproj_ln_kernel(a_ref, w_ref, h_ref, g_ref, b_ref, o32_ref, o16_ref):
    mix = jnp.dot(a_ref[...], w_ref[...], preferred_element_type=F32)
    out = _deepnorm_layer_norm(h_ref[...], mix, g_ref[...], b_ref[...])
    o32_ref[...] = out
    o16_ref[...] = out.astype(BF16)


def _outproj_ln(a16, w16, h32, g, b):
    n, k = a16.shape
    d = w16.shape[1]
    tm = min(ROW_TILE, n)
    return pl.pallas_call(
        _outproj_ln_kernel,
        grid=(n // tm,),
        in_specs=[pl.BlockSpec((tm, k), lambda i: (i, 0)),
                  pl.BlockSpec((k, d), lambda i: (0, 0)),
                  pl.BlockSpec((tm, d), lambda i: (i, 0)),
                  pl.BlockSpec((1, d), lambda i: (0, 0)),
                  pl.BlockSpec((1, d), lambda i: (0, 0))],
        out_specs=[pl.BlockSpec((tm, d), lambda i: (i, 0)),
                   pl.BlockSpec((tm, d), lambda i: (i, 0))],
        out_shape=[jax.ShapeDtypeStruct((n, d), F32),
                   jax.ShapeDtypeStruct((n, d), BF16)],
        compiler_params=_params("arbitrary"),
        name="outproj_ln",
    )(a16, w16, h32, g.reshape(1, d), b.reshape(1, d))


def _sb_kernel(q_ref, k_ref, v_ref, o_ref, acc_ref, surv_ref):
    t = SB_TILE
    i = pl.program_id(2)
    lane = lax.broadcasted_iota(jnp.int32, (t, LANES), 1)
    first_head = lane < SB_HEAD_DIM
    q = q_ref[...]
    zeros = jnp.zeros_like(q)
    q_heads = (jnp.where(first_head, q, zeros), jnp.where(first_head, zeros, q))
    row = lax.broadcasted_iota(jnp.int32, (t, t), 0)
    col = lax.broadcasted_iota(jnp.int32, (t, t), 1)
    strictly_before = col < row
    uj = lax.broadcasted_iota(jnp.int32, (t, 2 * t), 0)
    us = lax.broadcasted_iota(jnp.int32, (t, 2 * t), 1)
    suffix_and_ones = jnp.where((us >= t) | (uj > us), 1.0, 0.0).astype(BF16)

    def tile(head, kblk, vblk, diagonal):
        z = lax.dot_general(q_heads[head], kblk, (((1,), (1,)), ((), ())),
                            preferred_element_type=F32)
        softplus_z = jnp.maximum(z, 0.0) + jnp.log1p(jnp.exp(-jnp.abs(z)))
        log_1m = -softplus_z
        log_beta = z - softplus_z
        if diagonal:
            log_1m = jnp.where(strictly_before, log_1m, 0.0)
        hi = log_1m.astype(BF16)
        lo = (log_1m - hi.astype(F32)).astype(BF16)
        sums = (jnp.dot(hi, suffix_and_ones, preferred_element_type=F32)
                + jnp.dot(lo, suffix_and_ones, preferred_element_type=F32))
        later = sums[:, :t]
        row_sum = sums[:, t:]
        if diagonal:
            a = jnp.where(strictly_before, jnp.exp(log_beta + later), 0.0)
            acc_ref[head] = jnp.dot(a.astype(BF16), vblk, preferred_element_type=F32)
            surv_ref[head] = row_sum
        else:
            surv = surv_ref[head]
            a = jnp.exp(log_beta + later + surv)
            acc_ref[head] += jnp.dot(a.astype(BF16), vblk, preferred_element_type=F32)
            surv_ref[head] = surv + row_sum

    start = pl.multiple_of(i * t, t)
    kd = k_ref[pl.ds(start, t), :]
    vd = v_ref[pl.ds(start, t), :]
    tile(0, kd, vd, True)
    tile(1, kd, vd, True)

    def body(step, carry):
        s = pl.multiple_of((i - 1 - step) * t, t)
        kb = k_ref[pl.ds(s, t), :]
        vb = v_ref[pl.ds(s, t), :]
        tile(0, kb, vb, False)
        tile(1, kb, vb, False)
        return carry

    lax.fori_loop(0, i, body, 0)
    o_ref[...] = jnp.where(first_head, acc_ref[0], acc_ref[1]).astype(o_ref.dtype)


def _sb_attention(qkv16, batch, seq):
    t = SB_TILE
    d = D_MODEL
    pairs = d // LANES
    qkv = qkv16.reshape(batch, seq, 3 * d)
    out = pl.pallas_call(
        _sb_kernel,
        grid=(batch, pairs, seq // t),
        in_specs=[pl.BlockSpec((None, t, LANES), lambda b, p, i: (b, i, p)),
                  pl.BlockSpec((None, seq, LANES), lambda b, p, i: (b, 0, pairs + p)),
                  pl.BlockSpec((None, seq, LANES), lambda b, p, i: (b, 0, 2 * pairs + p))],
        out_specs=pl.BlockSpec((None, t, LANES), lambda b, p, i: (b, i, p)),
        out_shape=jax.ShapeDtypeStruct((batch, seq, d), BF16),
        scratch_shapes=[pltpu.VMEM((2, t, LANES), F32), pltpu.VMEM((2, t, LANES), F32)],
        compiler_params=_params("arbitrary", "arbitrary", "arbitrary"),
        name="sb_attention",
    )(qkv, qkv, qkv)
    return out.reshape(batch * seq, d)


def _conv_kernel(gb_ref, gc_ref, hh_ref, pc_ref, ph_ref, cw_ref, w_ref, h_ref, g_ref, b_ref,
                 o32_ref, o16_ref, *, blocks_per_seq):
    i = pl.program_id(0)
    u = gc_ref[...] * hh_ref[...]
    halo = pc_ref[...] * ph_ref[...]
    halo = jnp.where(i % blocks_per_seq == 0, jnp.zeros_like(halo), halo)
    prev1 = halo[SUBLANES - 1:SUBLANES, :]
    prev2 = halo[SUBLANES - 2:SUBLANES - 1, :]
    row = lax.broadcasted_iota(jnp.int32, u.shape, 0)
    u1 = jnp.where(row == 0, prev1, pltpu.roll(u, 1, 0))
    u2 = jnp.where(row == 0, prev2, jnp.where(row == 1, prev1, pltpu.roll(u, 2, 0)))
    conv = cw_ref[0:1, :] * u2 + cw_ref[1:2, :] * u1 + cw_ref[2:3, :] * u
    gated = (gb_ref[...] * conv).astype(BF16)
    mix = jnp.dot(gated, w_ref[...], preferred_element_type=F32)
    out = _deepnorm_layer_norm(h_ref[...], mix, g_ref[...], b_ref[...])
    o32_ref[...] = out
    o16_ref[...] = out.astype(BF16)


def _conv_outproj_ln(proj32, conv_w, w16, h32, g, b, seq):
    n = proj32.shape[0]
    d = D_MODEL
    tm = min(ROW_TILE, seq)
    halo_blocks = tm // SUBLANES
    kern = functools.partial(_conv_kernel, blocks_per_seq=seq // tm)
    return pl.pallas_call(
        kern,
        grid=(n // tm,),
        in_specs=[pl.BlockSpec((tm, d), lambda i: (i, 0)),
                  pl.BlockSpec((tm, d), lambda i: (i, 1)),
                  pl.BlockSpec((tm, d), lambda i: (i, 2)),
                  pl.BlockSpec((SUBLANES, d), lambda i: (jnp.maximum(i * halo_blocks - 1, 0), 1)),
                  pl.BlockSpec((SUBLANES, d), lambda i: (jnp.maximum(i * halo_blocks - 1, 0), 2)),
                  pl.BlockSpec((SC_WIDTH, d), lambda i: (0, 0)),
                  pl.BlockSpec((d, d), lambda i: (0, 0)),
                  pl.BlockSpec((tm, d), lambda i: (i, 0)),
                  pl.BlockSpec((1, d), lambda i: (0, 0)),
                  pl.BlockSpec((1, d), lambda i: (0, 0))],
        out_specs=[pl.BlockSpec((tm, d), lambda i: (i, 0)),
                   pl.BlockSpec((tm, d), lambda i: (i, 0))],
        out_shape=[jax.ShapeDtypeStruct((n, d), F32),
                   jax.ShapeDtypeStruct((n, d), BF16)],
        compiler_params=_params("arbitrary"),
        name="conv_outproj_ln",
    )(proj32, proj32, proj32, proj32, proj32, conv_w, w16, h32, g.reshape(1, d), b.reshape(1, d))


def _rms_norm(x, g):
    ms = jnp.mean(x * x, axis=-1, keepdims=True)
    return x * lax.rsqrt(ms + RMS_EPS) * g


def _mla_prep_kernel(x_ref, pos_ref, freq_ref, sign_ref, win_ref, qn_ref, kvn_ref, wuq_ref, wukv_ref,
                     q_ref, k_ref, v_ref):
    lat = jnp.dot(x_ref[...], win_ref[...], preferred_element_type=F32)
    q_lat = lat[:, :Q_LORA]
    kv_lat = lat[:, Q_LORA:Q_LORA + KV_LORA]
    r0 = Q_LORA + KV_LORA
    k_r = lat[:, r0:r0 + QK_ROPE]
    k_r_swapped = lat[:, r0 + QK_ROPE:r0 + 2 * QK_ROPE]
    qn = _rms_norm(q_lat, qn_ref[...]).astype(BF16)
    kvn = _rms_norm(kv_lat, kvn_ref[...]).astype(BF16)
    qq = jnp.dot(qn, wuq_ref[...], preferred_element_type=F32)
    kv = jnp.dot(kvn, wukv_ref[...], preferred_element_type=F32)
    ang = pos_ref[...].astype(F32) * freq_ref[...]
    cc = jnp.cos(ang)
    ss = jnp.sin(ang) * sign_ref[...]
    k_rope = (k_r * cc + k_r_swapped * ss).astype(BF16)
    pad = jnp.zeros((x_ref.shape[0], MLA_QK_PAD - QK_NOPE - QK_ROPE), BF16)
    for h in range(MLA_HEADS):
        c = h * MLA_QK_PAD
        q_nope = qq[:, c:c + QK_NOPE]
        q_r = qq[:, c + QK_NOPE:c + QK_NOPE + QK_ROPE]
        q_r_swapped = qq[:, c + QK_NOPE + QK_ROPE:c + MLA_QK_PAD]
        q_rope = q_r * cc + q_r_swapped * ss
        q_ref[h, :, 0:QK_NOPE] = (q_nope * MLA_SCALE).astype(BF16)
        q_ref[h, :, QK_NOPE:QK_NOPE + QK_ROPE] = (q_rope * MLA_SCALE).astype(BF16)
        q_ref[h, :, QK_NOPE + QK_ROPE:MLA_QK_PAD] = pad
        c2 = h * (QK_NOPE + V_HEAD)
        k_ref[h, :, 0:QK_NOPE] = kv[:, c2:c2 + QK_NOPE].astype(BF16)
        k_ref[h, :, QK_NOPE:QK_NOPE + QK_ROPE] = k_rope
        k_ref[h, :, QK_NOPE + QK_ROPE:MLA_QK_PAD] = pad
        v_ref[h] = kv[:, c2 + QK_NOPE:c2 + QK_NOPE + V_HEAD].astype(BF16)


def _swap_halves(w):
    half = w.shape[-1] // 2
    return jnp.concatenate([w[..., half:], w[..., :half]], axis=-1)


def _mla_prep(h16, positions, w_in, q_norm, w_uq, kv_norm, w_ukv, batch, seq):
    n = h16.shape[0]
    d = D_MODEL
    tm = min(ROW_TILE, seq)
    r0 = Q_LORA + KV_LORA
    win = jnp.concatenate([w_in, _swap_halves(w_in[:, r0:r0 + QK_ROPE])], axis=1).astype(BF16)
    wq = w_uq.reshape(Q_LORA, MLA_HEADS, QK_NOPE + QK_ROPE)
    wq = jnp.concatenate([wq, _swap_halves(wq[:, :, QK_NOPE:])], axis=2)
    wq = wq.reshape(Q_LORA, MLA_HEADS * MLA_QK_PAD).astype(BF16)
    wkv = w_ukv.astype(BF16)
    inv_freq = ROPE_THETA ** (-jnp.arange(0, QK_ROPE, 2, dtype=F32) / QK_ROPE)
    freq = jnp.concatenate([inv_freq, inv_freq]).reshape(1, QK_ROPE)
    sign = jnp.concatenate([-jnp.ones((QK_ROPE // 2,), F32),
                            jnp.ones((QK_ROPE // 2,), F32)]).reshape(1, QK_ROPE)
    bps = seq // tm
    head_spec = lambda width: pl.BlockSpec((None, MLA_HEADS, tm, width),
                                           lambda i: (i // bps, 0, i % bps, 0))
    const = lambda shape: pl.BlockSpec(shape, lambda i: (0,) * len(shape))
    return pl.pallas_call(
        _mla_prep_kernel,
        grid=(n // tm,),
        in_specs=[pl.BlockSpec((tm, d), lambda i: (i, 0)),
                  pl.BlockSpec((tm, 1), lambda i: (i, 0)),
                  const((1, QK_ROPE)), const((1, QK_ROPE)),
                  const(win.shape), const((1, Q_LORA)), const((1, KV_LORA)),
                  const(wq.shape), const(wkv.shape)],
        out_specs=[head_spec(MLA_QK_PAD), head_spec(MLA_QK_PAD), head_spec(V_HEAD)],
        out_shape=[jax.ShapeDtypeStruct((batch, MLA_HEADS, seq, MLA_QK_PAD), BF16),
                   jax.ShapeDtypeStruct((batch, MLA_HEADS, seq, MLA_QK_PAD), BF16),
                   jax.ShapeDtypeStruct((batch, MLA_HEADS, seq, V_HEAD), BF16)],
        compiler_params=_params("arbitrary"),
        name="mla_prep",
    )(h16, positions.reshape(n, 1), freq, sign, win, q_norm.reshape(1, Q_LORA),
      kv_norm.reshape(1, KV_LORA), wq, wkv)


def _mla_attn_kernel(q_ref, k_ref, v_ref, o_ref, m_ref, l_ref, acc_ref):
    t = MLA_TILE
    i = pl.program_id(2)
    q = q_ref[...]
    m_ref[...] = jnp.full(m_ref.shape, -jnp.inf, F32)
    l_ref[...] = jnp.zeros(l_ref.shape, F32)
    acc_ref[...] = jnp.zeros(acc_ref.shape, F32)

    def step(start, diagonal):
        kb = k_ref[pl.ds(start, t), :]
        vb = v_ref[pl.ds(start, t), :]
        s = lax.dot_general(q, kb, (((1,), (1,)), ((), ())), preferred_element_type=F32)
        if diagonal:
            row = lax.broadcasted_iota(jnp.int32, (t, t), 0)
            col = lax.broadcasted_iota(jnp.int32, (t, t), 1)
            s = jnp.where(col <= row, s, -jnp.inf)
        m_prev = m_ref[...]
        m_new = jnp.maximum(m_prev, jnp.max(s, axis=1, keepdims=True))
        scale = jnp.exp(m_prev - m_new)
        p = jnp.exp(s - m_new)
        l_ref[...] = scale * l_ref[...] + jnp.sum(p, axis=1, keepdims=True)
        acc_ref[...] = scale * acc_ref[...] + jnp.dot(p.astype(BF16), vb,
                                                      preferred_element_type=F32)
        m_ref[...] = m_new

    def body(j, carry):
        step(pl.multiple_of(j * t, t), False)
        return carry

    lax.fori_loop(0, i, body, 0)
    step(pl.multiple_of(i * t, t), True)
    o_ref[...] = (acc_ref[...] / l_ref[...]).astype(o_ref.dtype)


def _mla_attention(q, k, v):
    batch, heads, seq, _ = q.shape
    t = MLA_TILE
    out = pl.pallas_call(
        _mla_attn_kernel,
        grid=(batch, heads, seq // t),
        in_specs=[pl.BlockSpec((None, None, t, MLA_QK_PAD), lambda b, h, i: (b, h, i, 0)),
                  pl.BlockSpec((None, None, seq, MLA_QK_PAD), lambda b, h, i: (b, h, 0, 0)),
                  pl.BlockSpec((None, None, seq, V_HEAD), lambda b, h, i: (b, h, 0, 0))],
        out_specs=pl.BlockSpec((None, t, V_HEAD), lambda b, h, i: (b, i, h)),
        out_shape=jax.ShapeDtypeStruct((batch, seq, heads * V_HEAD), BF16),
        scratch_shapes=[pltpu.VMEM((t, 1), F32), pltpu.VMEM((t, 1), F32),
                        pltpu.VMEM((t, V_HEAD), F32)],
        compiler_params=_params("arbitrary", "arbitrary", "arbitrary"),
        name="mla_attention",
    )(q, k, v)
    return out.reshape(batch * seq, heads * V_HEAD)


def _router_kernel(h_ref, whi_ref, wlo_ref, b_ref, idx_ref, gate_ref):
    x = h_ref[...]
    xhi = x.astype(BF16)
    xlo = (x - xhi.astype(F32)).astype(BF16)
    logits = (jnp.dot(xhi, whi_ref[...], preferred_element_type=F32)
              + jnp.dot(xhi, wlo_ref[...], preferred_element_type=F32)
              + jnp.dot(xlo, whi_ref[...], preferred_element_type=F32)) + b_ref[...]
    lane = lax.broadcasted_iota(jnp.int32, logits.shape, 1)
    lane_f = lane.astype(F32)
    vals = jnp.where(lane < N_EXPERTS, logits, -jnp.inf)
    idx_out = jnp.zeros(logits.shape, jnp.int32)
    gate_out = jnp.zeros(logits.shape, F32)
    top = None
    denom = None
    for k in range(TOP_K):
        m = jnp.max(vals, axis=1, keepdims=True)
        pick = jnp.min(jnp.where(vals == m, lane_f, float(LANES)), axis=1, keepdims=True)
        pick_i = pick.astype(jnp.int32)
        if k == 0:
            top = m
            e = jnp.ones_like(m)
            denom = e
        else:
            e = jnp.exp(m - top)
            denom = denom + e
        idx_out = jnp.where(lane == k, pick_i, idx_out)
        gate_out = jnp.where(lane == k, e, gate_out)
        vals = jnp.where(lane == pick_i, -jnp.inf, vals)
    idx_ref[...] = idx_out
    gate_ref[...] = gate_out / denom


def _router(h32, router_w, router_b):
    n, d = h32.shape
    tm = min(ROW_TILE, n)
    w = jnp.zeros((d, LANES), F32).at[:, :N_EXPERTS].set(router_w)
    whi = w.astype(BF16)
    wlo = (w - whi.astype(F32)).astype(BF16)
    b = jnp.zeros((1, LANES), F32).at[0, :N_EXPERTS].set(router_b)
    return pl.pallas_call(
        _router_kernel,
        grid=(n // tm,),
        in_specs=[pl.BlockSpec((tm, d), lambda i: (i, 0)),
                  pl.BlockSpec((d, LANES), lambda i: (0, 0)),
                  pl.BlockSpec((d, LANES), lambda i: (0, 0)),
                  pl.BlockSpec((1, LANES), lambda i: (0, 0))],
        out_specs=[pl.BlockSpec((tm, LANES), lambda i: (i, 0)),
                   pl.BlockSpec((tm, LANES), lambda i: (i, 0))],
        out_shape=[jax.ShapeDtypeStruct((n, LANES), jnp.int32),
                   jax.ShapeDtypeStruct((n, LANES), F32)],
        compiler_params=_params("arbitrary"),
        name="router",
    )(h32, whi, wlo, b)


def _routing_tables(top_idx, n):
    blk = MOE_ROWS
    n_rows = n * TOP_K + N_EXPERTS * blk
    n_blocks = n_rows // blk
    flat_e = top_idx.reshape(-1)
    order = jnp.argsort(flat_e, stable=True).astype(jnp.int32)
    sorted_e = flat_e[order]
    counts = jnp.bincount(flat_e, length=N_EXPERTS).astype(jnp.int32)
    padded = (counts + blk - 1) // blk * blk
    group_start = jnp.cumsum(counts) - counts
    padded_end = jnp.cumsum(padded)
    padded_start = padded_end - padded
    dest_sorted = padded_start[sorted_e] + (jnp.arange(n * TOP_K, dtype=jnp.int32) - group_start[sorted_e])
    row_tok = jnp.zeros((n_rows,), jnp.int32).at[dest_sorted].set(order // TOP_K)
    dest = jnp.zeros((n * TOP_K,), jnp.int32).at[order].set(dest_sorted)
    block_expert = jnp.minimum(
        jnp.searchsorted(padded_end, jnp.arange(n_blocks, dtype=jnp.int32) * blk, side="right"),
        N_EXPERTS - 1).astype(jnp.int32)
    blocks_used = (padded_end[-1] // blk).astype(jnp.int32).reshape(1)
    return row_tok, dest, block_expert, blocks_used


def _moe_kernel(be_ref, rt_ref, nb_ref, x_hbm, wg_ref, bg_ref, wu_ref, bu_ref, wd_ref, bd_ref,
                y_ref, buf_ref, sem_ref):
    tm = MOE_ROWS
    b = pl.program_id(0)
    used = nb_ref[0]

    def row_copy(block, r, slot):
        tok = rt_ref[block * tm + r]
        return pltpu.make_async_copy(x_hbm.at[pl.ds(tok, 1), :],
                                     buf_ref.at[slot, pl.ds(r, 1), :],
                                     sem_ref.at[slot])

    def start_gather(block, slot):
        def issue(r, carry):
            row_copy(block, r, slot).start()
            return carry
        lax.fori_loop(0, tm, issue, 0, unroll=8)

    def wait_gather(slot):
        pltpu.make_async_copy(x_hbm.at[pl.ds(0, tm), :], buf_ref.at[slot], sem_ref.at[slot]).wait()

    slot = b % 2

    @pl.when(jnp.logical_and(b == 0, used > 0))
    def _():
        start_gather(0, 0)

    @pl.when(b + 1 < used)
    def _():
        start_gather(b + 1, 1 - slot)

    @pl.when(b < used)
    def _():
        wait_gather(slot)
        xb = buf_ref[slot].astype(BF16)
        g = jnp.minimum(jnp.dot(xb, wg_ref[...], preferred_element_type=F32) + bg_ref[...],
                        SWIGLU_LIMIT)
        u = jnp.clip(jnp.dot(xb, wu_ref[...], preferred_element_type=F32) + bu_ref[...],
                     -SWIGLU_LIMIT, SWIGLU_LIMIT)
        hdn = g * jax.nn.sigmoid(SWIGLU_ALPHA * g) * (u + 1.0)
        y_ref[...] = jnp.dot(hdn.astype(BF16), wd_ref[...], preferred_element_type=F32) + bd_ref[...]

    @pl.when(b >= used)
    def _():
        y_ref[...] = jnp.zeros(y_ref.shape, F32)


def _moe_experts(h32, row_tok, block_expert, blocks_used, wg16, bg, wu16, bu, wd16, bd):
    n, d = h32.shape
    tm = MOE_ROWS
    n_rows = row_tok.shape[0]
    n_blocks = n_rows // tm
    de = wg16.shape[2]
    wspec = lambda rows, cols: pl.BlockSpec((None, rows, cols), lambda b, be, rt, nb: (be[b], 0, 0))
    grid_spec = pltpu.PrefetchScalarGridSpec(
        num_scalar_prefetch=3,
        grid=(n_blocks,),
        in_specs=[pl.BlockSpec(memory_space=pl.ANY),
                  wspec(d, de), wspec(1, de), wspec(d, de), wspec(1, de), wspec(de, d), wspec(1, d)],
        out_specs=pl.BlockSpec((tm, d), lambda b, be, rt, nb: (b, 0)),
        scratch_shapes=[pltpu.VMEM((2, tm, d), F32), pltpu.SemaphoreType.DMA((2,))],
    )
    return pl.pallas_call(
        _moe_kernel,
        grid_spec=grid_spec,
        out_shape=jax.ShapeDtypeStruct((n_rows, d), F32),
        compiler_params=_params("arbitrary"),
        name="moe_experts",
    )(block_expert, row_tok, blocks_used, h32, wg16, bg.reshape(N_EXPERTS, 1, de),
      wu16, bu.reshape(N_EXPERTS, 1, de), wd16, bd.reshape(N_EXPERTS, 1, d))


def _combine_kernel(dest_ref, y_hbm, gate_ref, h_ref, g_ref, b_ref, o32_ref, o16_ref,
                    buf_ref, sem_ref):
    tb = COMBINE_ROWS
    i = pl.program_id(0)
    steps = pl.num_programs(0)

    def start_gather(block, slot):
        def issue(t, carry):
            for k in range(TOP_K):
                row = dest_ref[(block * tb + t) * TOP_K + k]
                pltpu.make_async_copy(y_hbm.at[pl.ds(row, 1), :],
                                      buf_ref.at[slot, k, pl.ds(t, 1), :],
                                      sem_ref.at[slot]).start()
            return carry
        lax.fori_loop(0, tb, issue, 0, unroll=2)

    def wait_gather(slot):
        for k in range(TOP_K):
            pltpu.make_async_copy(y_hbm.at[pl.ds(0, tb), :], buf_ref.at[slot, k],
                                  sem_ref.at[slot]).wait()

    slot = i % 2

    @pl.when(i == 0)
    def _():
        start_gather(0, 0)

    @pl.when(i + 1 < steps)
    def _():
        start_gather(i + 1, 1 - slot)

    wait_gather(slot)
    gates = gate_ref[...]
    ffn = buf_ref[slot, 0] * gates[:, 0:1]
    for k in range(1, TOP_K):
        ffn = ffn + buf_ref[slot, k] * gates[:, k:k + 1]
    out = _deepnorm_layer_norm(h_ref[...], ffn, g_ref[...], b_ref[...])
    o32_ref[...] = out
    o16_ref[...] = out.astype(BF16)


def _combine_ln(y, dest, gates, h32, g, b):
    n, d = h32.shape
    tb = COMBINE_ROWS
    grid_spec = pltpu.PrefetchScalarGridSpec(
        num_scalar_prefetch=1,
        grid=(n // tb,),
        in_specs=[pl.BlockSpec(memory_space=pl.ANY),
                  pl.BlockSpec((tb, LANES), lambda i, dest: (i, 0)),
                  pl.BlockSpec((tb, d), lambda i, dest: (i, 0)),
                  pl.BlockSpec((1, d), lambda i, dest: (0, 0)),
                  pl.BlockSpec((1, d), lambda i, dest: (0, 0))],
        out_specs=[pl.BlockSpec((tb, d), lambda i, dest: (i, 0)),
                   pl.BlockSpec((tb, d), lambda i, dest: (i, 0))],
        scratch_shapes=[pltpu.VMEM((2, TOP_K, tb, d), F32), pltpu.SemaphoreType.DMA((2,))],
    )
    return pl.pallas_call(
        _combine_kernel,
        grid_spec=grid_spec,
        out_shape=[jax.ShapeDtypeStruct((n, d), F32), jax.ShapeDtypeStruct((n, d), BF16)],
        compiler_params=_params("arbitrary"),
        name="combine_ln",
    )(dest, y, gates, h32, g.reshape(1, d), b.reshape(1, d))


def _routed_ffn_ln(h32, router_w, router_b, w_gate, b_gate, w_up, b_up, w_down, b_down, g, b):
    n = h32.shape[0]
    idx_pad, gate_pad = _router(h32, router_w, router_b)
    row_tok, dest, block_expert, blocks_used = _routing_tables(idx_pad[:, :TOP_K], n)
    y = _moe_experts(h32, row_tok, block_expert, blocks_used,
                     w_gate.astype(BF16), b_gate, w_up.astype(BF16), b_up,
                     w_down.astype(BF16), b_down)
    return _combine_ln(y, dest, gate_pad, h32, g, b)


def kernel(x, positions, ln_g, ln_b, sb_w_in, sb_w_out, sc_w_in, sc_conv_w, sc_w_out, mla_w_in, mla_q_norm, mla_w_uq, mla_kv_norm, mla_w_ukv, mla_w_out, router_w, router_b, moe_w_gate, moe_b_gate, moe_w_up, moe_b_up, moe_w_down, moe_b_down):
    batch, seq, d = x.shape
    n = batch * seq
    h32 = x.reshape(n, d)
    h16 = h32.astype(BF16)
    for layer in range(DEPTH):
        kind, slot = layer % N_MIXERS, layer // N_MIXERS
        g0, b0 = ln_g[layer, 0], ln_b[layer, 0]
        if kind == 0:
            w_in = jnp.concatenate([sb_w_in[slot][:, :d] * SB_SCALE, sb_w_in[slot][:, d:]], axis=1)
            qkv = _proj(h16, w_in.astype(BF16), BF16, tn=d)
            attn = _sb_attention(qkv, batch, seq)
            h32, h16 = _outproj_ln(attn, sb_w_out[slot].astype(BF16), h32, g0, b0)
        elif kind == 1:
            proj = _proj(h16, sc_w_in[slot].astype(BF16), F32, tn=d)
            h32, h16 = _conv_outproj_ln(proj, sc_conv_w[slot], sc_w_out[slot].astype(BF16),
                                        h32, g0, b0, seq)
        else:
            q, k, v = _mla_prep(h16, positions, mla_w_in[slot], mla_q_norm[slot], mla_w_uq[slot],
                                mla_kv_norm[slot], mla_w_ukv[slot], batch, seq)
            attn = _mla_attention(q, k, v)
            h32, h16 = _outproj_ln(attn, mla_w_out[slot].astype(BF16), h32, g0, b0)
        h32, h16 = _routed_ffn_ln(h32, router_w[layer], router_b[layer], moe_w_gate[layer],
                                  moe_b_gate[layer], moe_w_up[layer], moe_b_up[layer],
                                  moe_w_down[layer], moe_b_down[layer],
                                  ln_g[layer, 1], ln_b[layer, 1])
    return h32.reshape(batch, seq, d)
```

```python
import functools
import math

import jax
import jax.numpy as jnp
from jax import lax
from jax.experimental import pallas as pl
from jax.experimental.pallas import tpu as pltpu

F32 = jnp.float32
BF16 = jnp.bfloat16

D_MODEL = 1024
DEPTH = 4
N_MIXERS = 3
SB_HEADS = 16
SB_HEAD_DIM = D_MODEL // SB_HEADS
SB_SCALE = 1.0 / math.sqrt(SB_HEAD_DIM)
SC_WIDTH = 3
MLA_HEADS = 8
QK_NOPE = 128
QK_ROPE = 64
V_HEAD = 128
Q_LORA = 384
KV_LORA = 256
MLA_SCALE = 1.0 / math.sqrt(QK_NOPE + QK_ROPE)
ROPE_THETA = 10000.0
N_EXPERTS = 32
TOP_K = 4
SWIGLU_LIMIT = 7.0
SWIGLU_ALPHA = 1.702
LN_EPS = 1e-5
RMS_EPS = 1e-6
DEEPNORM_ALPHA = (2 * DEPTH) ** 0.25
LOG2E = math.log2(math.e)
SB_UNDERFLOW = 110.0

LANES = 128
SUBLANES = 8
VMEM_LIMIT_BYTES = 56 * 1024 * 1024

ROW_TILE = 512
SB_TILE = 256
MLA_TILE = 256
MLA_HEADS_PER_STEP = 2
MOE_ROWS = 256
COMBINE_ROWS = 128
MLA_QK_PAD = 256


def _params(*semantics):
    return pltpu.CompilerParams(dimension_semantics=semantics,
                                vmem_limit_bytes=VMEM_LIMIT_BYTES)


def _deepnorm_layer_norm(h, sub, g, b):
    y = DEEPNORM_ALPHA * h + sub
    mu = jnp.mean(y, axis=-1, keepdims=True)
    d = y - mu
    var = jnp.mean(d * d, axis=-1, keepdims=True)
    return d * lax.rsqrt(var + LN_EPS) * g + b


def _proj_kernel(x_ref, w_ref, o_ref):
    o_ref[...] = jnp.dot(x_ref[...], w_ref[...],
                         preferred_element_type=F32).astype(o_ref.dtype)


def _proj(x16, w16, out_dtype, tn):
    n, k = x16.shape
    nout = w16.shape[1]
    tm = min(ROW_TILE, n)
    return pl.pallas_call(
        _proj_kernel,
        grid=(nout // tn, n // tm),
        in_specs=[pl.BlockSpec((tm, k), lambda j, i: (i, 0)),
                  pl.BlockSpec((k, tn), lambda j, i: (0, j))],
        out_specs=pl.BlockSpec((tm, tn), lambda j, i: (i, j)),
        out_shape=jax.ShapeDtypeStruct((n, nout), out_dtype),
        compiler_params=_params("arbitrary", "arbitrary"),
        name="proj",
    )(x16, w16)


def _outproj_ln_kernel(a_ref, w_ref, h_ref, g_ref, b_ref, o32_ref, o16_ref):
    mix = jnp.dot(a_ref[...], w_ref[...], preferred_element_type=F32)
    out = _deepnorm_layer_norm(h_ref[...], mix, g_ref[...], b_ref[...])
    o32_ref[...] = out
    o16_ref[...] = out.astype(BF16)


def _outproj_ln(a16, w16, h32, g, b):
    n, k = a16.shape
    d = w16.shape[1]
    tm = min(ROW_TILE, n)
    return pl.pallas_call(
        _outproj_ln_kernel,
        grid=(n // tm,),
        in_specs=[pl.BlockSpec((tm, k), lambda i: (i, 0)),
                  pl.BlockSpec((k, d), lambda i: (0, 0)),
                  pl.BlockSpec((tm, d), lambda i: (i, 0)),
                  pl.BlockSpec((1, d), lambda i: (0, 0)),
                  pl.BlockSpec((1, d), lambda i: (0, 0))],
        out_specs=[pl.BlockSpec((tm, d), lambda i: (i, 0)),
                   pl.BlockSpec((tm, d), lambda i: (i, 0))],
        out_shape=[jax.ShapeDtypeStruct((n, d), F32),
                   jax.ShapeDtypeStruct((n, d), BF16)],
        compiler_params=_params("arbitrary"),
        name="outproj_ln",
    )(a16, w16, h32, g.reshape(1, d), b.reshape(1, d))


def _sb_kernel(q_ref, k_ref, v_ref, o_ref, acc_ref, surv_ref):
    t = SB_TILE
    i = pl.program_id(2)
    lane = lax.broadcasted_iota(jnp.int32, (t, LANES), 1)
    first_head = lane < SB_HEAD_DIM
    q = q_ref[...]
    zeros = jnp.zeros_like(q)
    q_heads = (jnp.where(first_head, q, zeros), jnp.where(first_head, zeros, q))
    row = lax.broadcasted_iota(jnp.int32, (t, t), 0)
    col = lax.broadcasted_iota(jnp.int32, (t, t), 1)
    strictly_before = col < row
    later_key = jnp.where(row > col, 1.0, 0.0).astype(BF16)

    def scores(head, kblk):
        z = lax.dot_general(q_heads[head], kblk, (((1,), (1,)), ((), ())),
                            preferred_element_type=F32)
        sp = jnp.maximum(z, 0.0) + jnp.log(1.0 + jnp.exp(-jnp.abs(z)))
        return z, sp

    def suffix_sums(parts):
        hi = [p.astype(BF16) for p in parts]
        lo = [(p - h.astype(F32)).astype(BF16) for p, h in zip(parts, hi)]
        stacked = jnp.concatenate(hi + lo, axis=0)
        sums = jnp.dot(stacked, later_key, preferred_element_type=F32)
        m = len(parts)
        return [sums[c * t:(c + 1) * t] + sums[(m + c) * t:(m + c + 1) * t] for c in range(m)]

    def diagonal(head, kblk, vblk):
        z, sp = scores(head, kblk)
        spm = jnp.where(strictly_before, sp, 0.0)
        (cs,) = suffix_sums([spm])
        a = jnp.where(strictly_before, jnp.exp(z - sp - cs), 0.0)
        acc_ref[head] = jnp.dot(a.astype(BF16), vblk, preferred_element_type=F32)
        surv_ref[head] = jnp.sum(spm, axis=1, keepdims=True)

    def diagonal_and_previous(head, kblk, vblk):
        z, sp = scores(head, kblk)
        spm = jnp.where(strictly_before, sp[:, t:], 0.0)
        cs_p, cs_d = suffix_sums([sp[:, :t], spm])
        surv_p = jnp.sum(spm, axis=1, keepdims=True)
        e = z - sp
        a = jnp.concatenate([jnp.exp(e[:, :t] - cs_p - surv_p),
                             jnp.where(strictly_before, jnp.exp(e[:, t:] - cs_d), 0.0)], axis=1)
        acc_ref[head] = jnp.dot(a.astype(BF16), vblk, preferred_element_type=F32)
        surv_ref[head] = surv_p + jnp.sum(sp[:, :t], axis=1, keepdims=True)

    def single(head, kblk, vblk):
        z, sp = scores(head, kblk)
        (cs,) = suffix_sums([sp])
        surv = surv_ref[head]
        a = jnp.exp(z - sp - cs - surv)
        acc_ref[head] += jnp.dot(a.astype(BF16), vblk, preferred_element_type=F32)
        surv_ref[head] = surv + jnp.sum(sp, axis=1, keepdims=True)

    def keep_going():
        least = jnp.minimum(jnp.min(surv_ref[0]), jnp.min(surv_ref[1]))
        return (least < SB_UNDERFLOW).astype(jnp.int32)

    @pl.when(i == 0)
    def _():
        for head in range(2):
            diagonal(head, k_ref[pl.ds(0, t), :], v_ref[pl.ds(0, t), :])

    @pl.when(i > 0)
    def _():
        start = pl.multiple_of((i - 1) * t, t)
        kb = k_ref[pl.ds(start, 2 * t), :]
        vb = v_ref[pl.ds(start, 2 * t), :]
        for head in range(2):
            diagonal_and_previous(head, kb, vb)

    def cond(carry):
        step, go = carry
        return jnp.logical_and(step < i - 1, go > 0)

    def body(carry):
        step, _ = carry
        s = pl.multiple_of((i - 2 - step) * t, t)
        kb = k_ref[pl.ds(s, t), :]
        vb = v_ref[pl.ds(s, t), :]
        for head in range(2):
            single(head, kb, vb)
        return step + 1, keep_going()

    lax.while_loop(cond, body, (jnp.int32(0), keep_going()))
    o_ref[...] = jnp.where(first_head, acc_ref[0], acc_ref[1]).astype(o_ref.dtype)


def _sb_attention(qkv16, batch, seq):
    t = SB_TILE
    d = D_MODEL
    pairs = d // LANES
    qkv = qkv16.reshape(batch, seq, 3 * d)
    out = pl.pallas_call(
        _sb_kernel,
        grid=(batch, pairs, seq // t),
        in_specs=[pl.BlockSpec((None, t, LANES), lambda b, p, i: (b, i, p)),
                  pl.BlockSpec((None, seq, LANES), lambda b, p, i: (b, 0, pairs + p)),
                  pl.BlockSpec((None, seq, LANES), lambda b, p, i: (b, 0, 2 * pairs + p))],
        out_specs=pl.BlockSpec((None, t, LANES), lambda b, p, i: (b, i, p)),
        out_shape=jax.ShapeDtypeStruct((batch, seq, d), BF16),
        scratch_shapes=[pltpu.VMEM((2, t, LANES), F32), pltpu.VMEM((2, t, 1), F32)],
        compiler_params=_params("arbitrary", "arbitrary", "arbitrary"),
        name="sb_attention",
    )(qkv, qkv, qkv)
    return out.reshape(batch * seq, d)


def _conv_kernel(gb_ref, gc_ref, hh_ref, pc_ref, ph_ref, cw_ref, w_ref, h_ref, g_ref, b_ref,
                 o32_ref, o16_ref, *, blocks_per_seq):
    i = pl.program_id(0)
    u = gc_ref[...] * hh_ref[...]
    halo = pc_ref[...] * ph_ref[...]
    halo = jnp.where(i % blocks_per_seq == 0, jnp.zeros_like(halo), halo)
    prev1 = halo[SUBLANES - 1:SUBLANES, :]
    prev2 = halo[SUBLANES - 2:SUBLANES - 1, :]
    row = lax.broadcasted_iota(jnp.int32, u.shape, 0)
    u1 = jnp.where(row == 0, prev1, pltpu.roll(u, 1, 0))
    u2 = jnp.where(row == 0, prev2, jnp.where(row == 1, prev1, pltpu.roll(u, 2, 0)))
    conv = cw_ref[0:1, :] * u2 + cw_ref[1:2, :] * u1 + cw_ref[2:3, :] * u
    gated = (gb_ref[...] * conv).astype(BF16)
    mix = jnp.dot(gated, w_ref[...], preferred_element_type=F32)
    out = _deepnorm_layer_norm(h_ref[...], mix, g_ref[...], b_ref[...])
    o32_ref[...] = out
    o16_ref[...] = out.astype(BF16)


def _conv_outproj_ln(proj32, conv_w, w16, h32, g, b, seq):
    n = proj32.shape[0]
    d = D_MODEL
    tm = min(ROW_TILE, seq)
    halo_blocks = tm // SUBLANES
    kern = functools.partial(_conv_kernel, blocks_per_seq=seq // tm)
    return pl.pallas_call(
        kern,
        grid=(n // tm,),
        in_specs=[pl.BlockSpec((tm, d), lambda i: (i, 0)),
                  pl.BlockSpec((tm, d), lambda i: (i, 1)),
                  pl.BlockSpec((tm, d), lambda i: (i, 2)),
                  pl.BlockSpec((SUBLANES, d), lambda i: (jnp.maximum(i * halo_blocks - 1, 0), 1)),
                  pl.BlockSpec((SUBLANES, d), lambda i: (jnp.maximum(i * halo_blocks - 1, 0), 2)),
                  pl.BlockSpec((SC_WIDTH, d), lambda i: (0, 0)),
                  pl.BlockSpec((d, d), lambda i: (0, 0)),
                  pl.BlockSpec((tm, d), lambda i: (i, 0)),
                  pl.BlockSpec((1, d), lambda i: (0, 0)),
                  pl.BlockSpec((1, d), lambda i: (0, 0))],
        out_specs=[pl.BlockSpec((tm, d), lambda i: (i, 0)),
                   pl.BlockSpec((tm, d), lambda i: (i, 0))],
        out_shape=[jax.ShapeDtypeStruct((n, d), F32),
                   jax.ShapeDtypeStruct((n, d), BF16)],
        compiler_params=_params("arbitrary"),
        name="conv_outproj_ln",
    )(proj32, proj32, proj32, proj32, proj32, conv_w, w16, h32, g.reshape(1, d), b.reshape(1, d))


def _rms_norm(x, g):
    ms = jnp.mean(x * x, axis=-1, keepdims=True)
    return x * lax.rsqrt(ms + RMS_EPS) * g


def _mla_prep_kernel(x_ref, pos_ref, freq_ref, sign_ref, win_ref, qn_ref, kvn_ref, wuq_ref, wukv_ref,
                     q_ref, k_ref, v_ref):
    lat = jnp.dot(x_ref[...], win_ref[...], preferred_element_type=F32)
    q_lat = lat[:, :Q_LORA]
    kv_lat = lat[:, Q_LORA:Q_LORA + KV_LORA]
    r0 = Q_LORA + KV_LORA
    k_r = lat[:, r0:r0 + QK_ROPE]
    k_r_swapped = lat[:, r0 + QK_ROPE:r0 + 2 * QK_ROPE]
    qn = _rms_norm(q_lat, qn_ref[...]).astype(BF16)
    kvn = _rms_norm(kv_lat, kvn_ref[...]).astype(BF16)
    qq = jnp.dot(qn, wuq_ref[...], preferred_element_type=F32)
    kv = jnp.dot(kvn, wukv_ref[...], preferred_element_type=F32)
    ang = pos_ref[...].astype(F32) * freq_ref[...]
    cc = jnp.cos(ang)
    ss = jnp.sin(ang) * sign_ref[...]
    k_rope = (k_r * cc + k_r_swapped * ss).astype(BF16)
    pad = jnp.zeros((x_ref.shape[0], MLA_QK_PAD - QK_NOPE - QK_ROPE), BF16)
    for h in range(MLA_HEADS):
        c = h * MLA_QK_PAD
        q_nope = qq[:, c:c + QK_NOPE]
        q_r = qq[:, c + QK_NOPE:c + QK_NOPE + QK_ROPE]
        q_r_swapped = qq[:, c + QK_NOPE + QK_ROPE:c + MLA_QK_PAD]
        q_rope = q_r * cc + q_r_swapped * ss
        q_ref[h, :, 0:QK_NOPE] = (q_nope * (MLA_SCALE * LOG2E)).astype(BF16)
        q_ref[h, :, QK_NOPE:QK_NOPE + QK_ROPE] = (q_rope * (MLA_SCALE * LOG2E)).astype(BF16)
        q_ref[h, :, QK_NOPE + QK_ROPE:MLA_QK_PAD] = pad
        c2 = h * (QK_NOPE + V_HEAD)
        k_ref[h, :, 0:QK_NOPE] = kv[:, c2:c2 + QK_NOPE].astype(BF16)
        k_ref[h, :, QK_NOPE:QK_NOPE + QK_ROPE] = k_rope
        k_ref[h, :, QK_NOPE + QK_ROPE:MLA_QK_PAD] = pad
        v_ref[h] = kv[:, c2 + QK_NOPE:c2 + QK_NOPE + V_HEAD].astype(BF16)


def _swap_halves(w):
    half = w.shape[-1] // 2
    return jnp.concatenate([w[..., half:], w[..., :half]], axis=-1)


def _mla_prep(h16, positions, w_in, q_norm, w_uq, kv_norm, w_ukv, batch, seq):
    n = h16.shape[0]
    d = D_MODEL
    tm = min(ROW_TILE, seq)
    r0 = Q_LORA + KV_LORA
    win = jnp.concatenate([w_in, _swap_halves(w_in[:, r0:r0 + QK_ROPE])], axis=1).astype(BF16)
    wq = w_uq.reshape(Q_LORA, MLA_HEADS, QK_NOPE + QK_ROPE)
    wq = jnp.concatenate([wq, _swap_halves(wq[:, :, QK_NOPE:])], axis=2)
    wq = wq.reshape(Q_LORA, MLA_HEADS * MLA_QK_PAD).astype(BF16)
    wkv = w_ukv.astype(BF16)
    inv_freq = ROPE_THETA ** (-jnp.arange(0, QK_ROPE, 2, dtype=F32) / QK_ROPE)
    freq = jnp.concatenate([inv_freq, inv_freq]).reshape(1, QK_ROPE)
    sign = jnp.concatenate([-jnp.ones((QK_ROPE // 2,), F32),
                            jnp.ones((QK_ROPE // 2,), F32)]).reshape(1, QK_ROPE)
    bps = seq // tm
    head_spec = lambda width: pl.BlockSpec((None, MLA_HEADS, tm, width),
                                           lambda i: (i // bps, 0, i % bps, 0))
    const = lambda shape: pl.BlockSpec(shape, lambda i: (0,) * len(shape))
    return pl.pallas_call(
        _mla_prep_kernel,
        grid=(n // tm,),
        in_specs=[pl.BlockSpec((tm, d), lambda i: (i, 0)),
                  pl.BlockSpec((tm, 1), lambda i: (i, 0)),
                  const((1, QK_ROPE)), const((1, QK_ROPE)),
                  const(win.shape), const((1, Q_LORA)), const((1, KV_LORA)),
                  const(wq.shape), const(wkv.shape)],
        out_specs=[head_spec(MLA_QK_PAD), head_spec(MLA_QK_PAD), head_spec(V_HEAD)],
        out_shape=[jax.ShapeDtypeStruct((batch, MLA_HEADS, seq, MLA_QK_PAD), BF16),
                   jax.ShapeDtypeStruct((batch, MLA_HEADS, seq, MLA_QK_PAD), BF16),
                   jax.ShapeDtypeStruct((batch, MLA_HEADS, seq, V_HEAD), BF16)],
        compiler_params=_params("arbitrary"),
        name="mla_prep",
    )(h16, positions.reshape(n, 1), freq, sign, win, q_norm.reshape(1, Q_LORA),
      kv_norm.reshape(1, KV_LORA), wq, wkv)


def _mla_attn_kernel(q_ref, k_ref, v_ref, o_ref, m_ref, l_ref, acc_ref):
    t = MLA_TILE
    i = pl.program_id(2)
    m_ref[...] = jnp.full(m_ref.shape, -jnp.inf, F32)
    l_ref[...] = jnp.zeros(l_ref.shape, F32)
    acc_ref[...] = jnp.zeros(acc_ref.shape, F32)

    def step(head, start, width, diagonal):
        kb = k_ref[head, pl.ds(start, width), :]
        vb = v_ref[head, pl.ds(start, width), :]
        s = lax.dot_general(q_ref[head], kb, (((1,), (1,)), ((), ())),
                            preferred_element_type=F32)
        if diagonal:
            row = lax.broadcasted_iota(jnp.int32, (t, width), 0)
            col = lax.broadcasted_iota(jnp.int32, (t, width), 1)
            s = jnp.where(col <= row, s, -jnp.inf)
        m_prev = m_ref[head]
        m_new = jnp.maximum(m_prev, jnp.max(s, axis=1, keepdims=True))
        scale = jnp.exp2(m_prev - m_new)
        p = jnp.exp2(s - m_new)
        l_ref[head] = scale * l_ref[head] + jnp.sum(p, axis=1, keepdims=True)
        acc_ref[head] = scale * acc_ref[head] + jnp.dot(p.astype(BF16), vb,
                                                        preferred_element_type=F32)
        m_ref[head] = m_new

    def body(j, carry):
        for head in range(MLA_HEADS_PER_STEP):
            step(head, pl.multiple_of(j * 2 * t, 2 * t), 2 * t, False)
        return carry

    lax.fori_loop(0, i // 2, body, 0)

    @pl.when(i % 2 == 1)
    def _():
        for head in range(MLA_HEADS_PER_STEP):
            step(head, pl.multiple_of((i - 1) * t, t), t, False)

    for head in range(MLA_HEADS_PER_STEP):
        step(head, pl.multiple_of(i * t, t), t, True)
        o_ref[:, head * V_HEAD:(head + 1) * V_HEAD] = (
            acc_ref[head] / l_ref[head]).astype(o_ref.dtype)


def _mla_attention(q, k, v):
    batch, heads, seq, _ = q.shape
    t = MLA_TILE
    hp = MLA_HEADS_PER_STEP
    out = pl.pallas_call(
        _mla_attn_kernel,
        grid=(batch, heads // hp, seq // t),
        in_specs=[pl.BlockSpec((None, hp, t, MLA_QK_PAD), lambda b, h, i: (b, h, i, 0)),
                  pl.BlockSpec((None, hp, seq, MLA_QK_PAD), lambda b, h, i: (b, h, 0, 0)),
                  pl.BlockSpec((None, hp, seq, V_HEAD), lambda b, h, i: (b, h, 0, 0))],
        out_specs=pl.BlockSpec((None, t, hp * V_HEAD), lambda b, h, i: (b, i, h)),
        out_shape=jax.ShapeDtypeStruct((batch, seq, heads * V_HEAD), BF16),
        scratch_shapes=[pltpu.VMEM((hp, t, 1), F32), pltpu.VMEM((hp, t, 1), F32),
                        pltpu.VMEM((hp, t, V_HEAD), F32)],
        compiler_params=_params("arbitrary", "arbitrary", "arbitrary"),
        name="mla_attention",
    )(q, k, v)
    return out.reshape(batch * seq, heads * V_HEAD)


def _router_kernel(h_ref, whi_ref, wlo_ref, b_ref, idx_ref, gate_ref):
    x = h_ref[...]
    xhi = x.astype(BF16)
    xlo = (x - xhi.astype(F32)).astype(BF16)
    logits = (jnp.dot(xhi, whi_ref[...], preferred_element_type=F32)
              + jnp.dot(xhi, wlo_ref[...], preferred_element_type=F32)
              + jnp.dot(xlo, whi_ref[...], preferred_element_type=F32)) + b_ref[...]
    lane = lax.broadcasted_iota(jnp.int32, logits.shape, 1)
    lane_f = lane.astype(F32)
    vals = jnp.where(lane < N_EXPERTS, logits, -jnp.inf)
    idx_out = jnp.zeros(logits.shape, jnp.int32)
    gate_out = jnp.zeros(logits.shape, F32)
    top = None
    denom = None
    for k in range(TOP_K):
        m = jnp.max(vals, axis=1, keepdims=True)
        pick = jnp.min(jnp.where(vals == m, lane_f, float(LANES)), axis=1, keepdims=True)
        pick_i = pick.astype(jnp.int32)
        if k == 0:
            top = m
            e = jnp.ones_like(m)
            denom = e
        else:
            e = jnp.exp(m - top)
            denom = denom + e
        idx_out = jnp.where(lane == k, pick_i, idx_out)
        gate_out = jnp.where(lane == k, e, gate_out)
        vals = jnp.where(lane == pick_i, -jnp.inf, vals)
    idx_ref[...] = idx_out
    gate_ref[...] = gate_out / denom


def _router(h32, router_w, router_b):
    n, d = h32.shape
    tm = min(ROW_TILE, n)
    w = jnp.zeros((d, LANES), F32).at[:, :N_EXPERTS].set(router_w)
    whi = w.astype(BF16)
    wlo = (w - whi.astype(F32)).astype(BF16)
    b = jnp.zeros((1, LANES), F32).at[0, :N_EXPERTS].set(router_b)
    return pl.pallas_call(
        _router_kernel,
        grid=(n // tm,),
        in_specs=[pl.BlockSpec((tm, d), lambda i: (i, 0)),
                  pl.BlockSpec((d, LANES), lambda i: (0, 0)),
                  pl.BlockSpec((d, LANES), lambda i: (0, 0)),
                  pl.BlockSpec((1, LANES), lambda i: (0, 0))],
        out_specs=[pl.BlockSpec((tm, LANES), lambda i: (i, 0)),
                   pl.BlockSpec((tm, LANES), lambda i: (i, 0))],
        out_shape=[jax.ShapeDtypeStruct((n, LANES), jnp.int32),
                   jax.ShapeDtypeStruct((n, LANES), F32)],
        compiler_params=_params("arbitrary"),
        name="router",
    )(h32, whi, wlo, b)


def _routing_tables(top_idx, n):
    blk = MOE_ROWS
    n_rows = n * TOP_K + N_EXPERTS * blk
    n_blocks = n_rows // blk
    flat_e = top_idx.reshape(-1)
    order = jnp.argsort(flat_e, stable=True).astype(jnp.int32)
    sorted_e = flat_e[order]
    counts = jnp.bincount(flat_e, length=N_EXPERTS).astype(jnp.int32)
    padded = (counts + blk - 1) // blk * blk
    group_start = jnp.cumsum(counts) - counts
    padded_end = jnp.cumsum(padded)
    padded_start = padded_end - padded
    dest_sorted = padded_start[sorted_e] + (jnp.arange(n * TOP_K, dtype=jnp.int32) - group_start[sorted_e])
    row_tok = jnp.zeros((n_rows,), jnp.int32).at[dest_sorted].set(order // TOP_K)
    dest = jnp.zeros((n * TOP_K,), jnp.int32).at[order].set(dest_sorted)
    block_expert = jnp.minimum(
        jnp.searchsorted(padded_end, jnp.arange(n_blocks, dtype=jnp.int32) * blk, side="right"),
        N_EXPERTS - 1).astype(jnp.int32)
    blocks_used = (padded_end[-1] // blk).astype(jnp.int32).reshape(1)
    return row_tok, dest, block_expert, blocks_used


def _moe_kernel(be_ref, rt_ref, nb_ref, x_hbm, wg_ref, bg_ref, wu_ref, bu_ref, wd_ref, bd_ref,
                y_ref, buf_ref, sem_ref):
    tm = MOE_ROWS
    b = pl.program_id(0)
    used = nb_ref[0]

    def row_copy(block, r, slot):
        tok = rt_ref[block * tm + r]
        return pltpu.make_async_copy(x_hbm.at[pl.ds(tok, 1), :],
                                     buf_ref.at[slot, pl.ds(r, 1), :],
                                     sem_ref.at[slot])

    def start_gather(block, slot):
        def issue(r, carry):
            row_copy(block, r, slot).start()
            return carry
        lax.fori_loop(0, tm, issue, 0, unroll=8)

    def wait_gather(slot):
        pltpu.make_async_copy(x_hbm.at[pl.ds(0, tm), :], buf_ref.at[slot], sem_ref.at[slot]).wait()

    slot = b % 2

    @pl.when(jnp.logical_and(b == 0, used > 0))
    def _():
        start_gather(0, 0)

    @pl.when(b + 1 < used)
    def _():
        start_gather(b + 1, 1 - slot)

    @pl.when(b < used)
    def _():
        wait_gather(slot)
        xb = buf_ref[slot].astype(BF16)
        g = jnp.minimum(jnp.dot(xb, wg_ref[...], preferred_element_type=F32) + bg_ref[...],
                        SWIGLU_LIMIT)
        u = jnp.clip(jnp.dot(xb, wu_ref[...], preferred_element_type=F32) + bu_ref[...],
                     -SWIGLU_LIMIT, SWIGLU_LIMIT)
        hdn = g * jax.nn.sigmoid(SWIGLU_ALPHA * g) * (u + 1.0)
        y_ref[...] = jnp.dot(hdn.astype(BF16), wd_ref[...], preferred_element_type=F32) + bd_ref[...]

    @pl.when(b >= used)
    def _():
        y_ref[...] = jnp.zeros(y_ref.shape, F32)


def _moe_experts(h32, row_tok, block_expert, blocks_used, wg16, bg, wu16, bu, wd16, bd):
    n, d = h32.shape
    tm = MOE_ROWS
    n_rows = row_tok.shape[0]
    n_blocks = n_rows // tm
    de = wg16.shape[2]
    wspec = lambda rows, cols: pl.BlockSpec((None, rows, cols), lambda b, be, rt, nb: (be[b], 0, 0))
    grid_spec = pltpu.PrefetchScalarGridSpec(
        num_scalar_prefetch=3,
        grid=(n_blocks,),
        in_specs=[pl.BlockSpec(memory_space=pl.ANY),
                  wspec(d, de), wspec(1, de), wspec(d, de), wspec(1, de), wspec(de, d), wspec(1, d)],
        out_specs=pl.BlockSpec((tm, d), lambda b, be, rt, nb: (b, 0)),
        scratch_shapes=[pltpu.VMEM((2, tm, d), F32), pltpu.SemaphoreType.DMA((2,))],
    )
    return pl.pallas_call(
        _moe_kernel,
        grid_spec=grid_spec,
        out_shape=jax.ShapeDtypeStruct((n_rows, d), F32),
        compiler_params=_params("arbitrary"),
        name="moe_experts",
    )(block_expert, row_tok, blocks_used, h32, wg16, bg.reshape(N_EXPERTS, 1, de),
      wu16, bu.reshape(N_EXPERTS, 1, de), wd16, bd.reshape(N_EXPERTS, 1, d))


def _combine_kernel(dest_ref, y_hbm, gate_ref, h_ref, g_ref, b_ref, o32_ref, o16_ref,
                    buf_ref, sem_ref):
    tb = COMBINE_ROWS
    i = pl.program_id(0)
    steps = pl.num_programs(0)

    def start_gather(block, slot):
        def issue(t, carry):
            for k in range(TOP_K):
                row = dest_ref[(block * tb + t) * TOP_K + k]
                pltpu.make_async_copy(y_hbm.at[pl.ds(row, 1), :],
                                      buf_ref.at[slot, k, pl.ds(t, 1), :],
                                      sem_ref.at[slot]).start()
            return carry
        lax.fori_loop(0, tb, issue, 0, unroll=2)

    def wait_gather(slot):
        for k in range(TOP_K):
            pltpu.make_async_copy(y_hbm.at[pl.ds(0, tb), :], buf_ref.at[slot, k],
                                  sem_ref.at[slot]).wait()

    slot = i % 2

    @pl.when(i == 0)
    def _():
        start_gather(0, 0)

    @pl.when(i + 1 < steps)
    def _():
        start_gather(i + 1, 1 - slot)

    wait_gather(slot)
    gates = gate_ref[...]
    ffn = buf_ref[slot, 0] * gates[:, 0:1]
    for k in range(1, TOP_K):
        ffn = ffn + buf_ref[slot, k] * gates[:, k:k + 1]
    out = _deepnorm_layer_norm(h_ref[...], ffn, g_ref[...], b_ref[...])
    o32_ref[...] = out
    o16_ref[...] = out.astype(BF16)


def _combine_ln(y, dest, gates, h32, g, b):
    n, d = h32.shape
    tb = COMBINE_ROWS
    grid_spec = pltpu.PrefetchScalarGridSpec(
        num_scalar_prefetch=1,
        grid=(n // tb,),
        in_specs=[pl.BlockSpec(memory_space=pl.ANY),
                  pl.BlockSpec((tb, LANES), lambda i, dest: (i, 0)),
                  pl.BlockSpec((tb, d), lambda i, dest: (i, 0)),
                  pl.BlockSpec((1, d), lambda i, dest: (0, 0)),
                  pl.BlockSpec((1, d), lambda i, dest: (0, 0))],
        out_specs=[pl.BlockSpec((tb, d), lambda i, dest: (i, 0)),
                   pl.BlockSpec((tb, d), lambda i, dest: (i, 0))],
        scratch_shapes=[pltpu.VMEM((2, TOP_K, tb, d), F32), pltpu.SemaphoreType.DMA((2,))],
    )
    return pl.pallas_call(
        _combine_kernel,
        grid_spec=grid_spec,
        out_shape=[jax.ShapeDtypeStruct((n, d), F32), jax.ShapeDtypeStruct((n, d), BF16)],
        compiler_params=_params("arbitrary"),
        name="combine_ln",
    )(dest, y, gates, h32, g.reshape(1, d), b.reshape(1, d))


def _routed_ffn_ln(h32, router_w, router_b, w_gate, b_gate, w_up, b_up, w_down, b_down, g, b):
    n = h32.shape[0]
    idx_pad, gate_pad = _router(h32, router_w, router_b)
    row_tok, dest, block_expert, blocks_used = _routing_tables(idx_pad[:, :TOP_K], n)
    y = _moe_experts(h32, row_tok, block_expert, blocks_used,
                     w_gate.astype(BF16), b_gate, w_up.astype(BF16), b_up,
                     w_down.astype(BF16), b_down)
    return _combine_ln(y, dest, gate_pad, h32, g, b)


def kernel(x, positions, ln_g, ln_b, sb_w_in, sb_w_out, sc_w_in, sc_conv_w, sc_w_out, mla_w_in, mla_q_norm, mla_w_uq, mla_kv_norm, mla_w_ukv, mla_w_out, router_w, router_b, moe_w_gate, moe_b_gate, moe_w_up, moe_b_up, moe_w_down, moe_b_down):
    batch, seq, d = x.shape
    n = batch * seq
    h32 = x.reshape(n, d)
    h16 = h32.astype(BF16)
    for layer in range(DEPTH):
        kind, slot = layer % N_MIXERS, layer // N_MIXERS
        g0, b0 = ln_g[layer, 0], ln_b[layer, 0]
        if kind == 0:
            w_in = jnp.concatenate([sb_w_in[slot][:, :d] * SB_SCALE, sb_w_in[slot][:, d:]], axis=1)
            qkv = _proj(h16, w_in.astype(BF16), BF16, tn=d)
            attn = _sb_attention(qkv, batch, seq)
            h32, h16 = _outproj_ln(attn, sb_w_out[slot].astype(BF16), h32, g0, b0)
        elif kind == 1:
            proj = _proj(h16, sc_w_in[slot].astype(BF16), F32, tn=d)
            h32, h16 = _conv_outproj_ln(proj, sc_conv_w[slot], sc_w_out[slot].astype(BF16),
                                        h32, g0, b0, seq)
        else:
            q, k, v = _mla_prep(h16, positions, mla_w_in[slot], mla_q_norm[slot], mla_w_uq[slot],
                                mla_kv_norm[slot], mla_w_ukv[slot], batch, seq)
            attn = _mla_attention(q, k, v)
            h32, h16 = _outproj_ln(attn, mla_w_out[slot].astype(BF16), h32, g0, b0)
        h32, h16 = _routed_ffn_ln(h32, router_w[layer], router_b[layer], moe_w_gate[layer],
                                  moe_b_gate[layer], moe_w_up[layer], moe_b_up[layer],
                                  moe_w_down[layer], moe_b_down[layer],
                                  ln_g[layer, 1], ln_b[layer, 1])
    return h32.reshape(batch, seq, d)
```

```python
import functools
import math

import jax
import jax.numpy as jnp
from jax import lax
from jax.experimental import pallas as pl
from jax.experimental.pallas import tpu as pltpu

F32 = jnp.float32
BF16 = jnp.bfloat16

D_MODEL = 1024
DEPTH = 4
N_MIXERS = 3
SB_HEADS = 16
SB_HEAD_DIM = D_MODEL // SB_HEADS
SB_SCALE = 1.0 / math.sqrt(SB_HEAD_DIM)
SC_WIDTH = 3
MLA_HEADS = 8
QK_NOPE = 128
QK_ROPE = 64
V_HEAD = 128
Q_LORA = 384
KV_LORA = 256
MLA_SCALE = 1.0 / math.sqrt(QK_NOPE + QK_ROPE)
ROPE_THETA = 10000.0
N_EXPERTS = 32
TOP_K = 4
SWIGLU_LIMIT = 7.0
SWIGLU_ALPHA = 1.702
LN_EPS = 1e-5
RMS_EPS = 1e-6
DEEPNORM_ALPHA = (2 * DEPTH) ** 0.25
LOG2E = math.log2(math.e)
SB_UNDERFLOW = 110.0

LANES = 128
SUBLANES = 8
VMEM_LIMIT_BYTES = 56 * 1024 * 1024

ROW_TILE = 512
SB_TILE = 256
MLA_TILE = 256
MLA_HEADS_PER_STEP = 2
MOE_ROWS = 256
COMBINE_ROWS = 128
DISPATCH_ROWS = 256
MLA_QK_PAD = 256


def _params(*semantics):
    return pltpu.CompilerParams(dimension_semantics=semantics,
                                vmem_limit_bytes=VMEM_LIMIT_BYTES)


def _deepnorm_layer_norm(h, sub, g, b):
    y = DEEPNORM_ALPHA * h + sub
    mu = jnp.mean(y, axis=-1, keepdims=True)
    d = y - mu
    var = jnp.mean(d * d, axis=-1, keepdims=True)
    return d * lax.rsqrt(var + LN_EPS) * g + b


def _proj_kernel(x_ref, w_ref, o_ref, w16_ref):
    @pl.when(pl.program_id(1) == 0)
    def _():
        w16_ref[...] = w_ref[...].astype(BF16)

    o_ref[...] = jnp.dot(x_ref[...], w16_ref[...],
                         preferred_element_type=F32).astype(o_ref.dtype)


def _proj(x16, w32, out_dtype, tn):
    n, k = x16.shape
    nout = w32.shape[1]
    tm = min(ROW_TILE, n)
    return pl.pallas_call(
        _proj_kernel,
        grid=(nout // tn, n // tm),
        in_specs=[pl.BlockSpec((tm, k), lambda j, i: (i, 0)),
                  pl.BlockSpec((k, tn), lambda j, i: (0, j))],
        out_specs=pl.BlockSpec((tm, tn), lambda j, i: (i, j)),
        out_shape=jax.ShapeDtypeStruct((n, nout), out_dtype),
        scratch_shapes=[pltpu.VMEM((k, tn), BF16)],
        compiler_params=_params("arbitrary", "arbitrary"),
        name="proj",
    )(x16, w32)


def _outproj_ln_kernel(a_ref, w_ref, h_ref, g_ref, b_ref, o32_ref, o16_ref, w16_ref):
    @pl.when(pl.program_id(0) == 0)
    def _():
        w16_ref[...] = w_ref[...].astype(BF16)

    mix = jnp.dot(a_ref[...], w16_ref[...], preferred_element_type=F32)
    out = _deepnorm_layer_norm(h_ref[...], mix, g_ref[...], b_ref[...])
    o32_ref[...] = out
    o16_ref[...] = out.astype(BF16)


def _outproj_ln(a16, w32, h32, g, b):
    n, k = a16.shape
    d = w32.shape[1]
    tm = min(ROW_TILE, n)
    return pl.pallas_call(
        _outproj_ln_kernel,
        grid=(n // tm,),
        in_specs=[pl.BlockSpec((tm, k), lambda i: (i, 0)),
                  pl.BlockSpec((k, d), lambda i: (0, 0)),
                  pl.BlockSpec((tm, d), lambda i: (i, 0)),
                  pl.BlockSpec((1, d), lambda i: (0, 0)),
                  pl.BlockSpec((1, d), lambda i: (0, 0))],
        out_specs=[pl.BlockSpec((tm, d), lambda i: (i, 0)),
                   pl.BlockSpec((tm, d), lambda i: (i, 0))],
        out_shape=[jax.ShapeDtypeStruct((n, d), F32),
                   jax.ShapeDtypeStruct((n, d), BF16)],
        scratch_shapes=[pltpu.VMEM((k, d), BF16)],
        compiler_params=_params("arbitrary"),
        name="outproj_ln",
    )(a16, w32, h32, g.reshape(1, d), b.reshape(1, d))


def _sb_kernel(q_ref, k_ref, v_ref, o_ref, acc_ref, surv_ref):
    t = SB_TILE
    i = pl.program_id(2)
    lane = lax.broadcasted_iota(jnp.int32, (t, LANES), 1)
    first_head = lane < SB_HEAD_DIM
    q = q_ref[...] * SB_SCALE
    zeros = jnp.zeros_like(q)
    q_heads = (jnp.where(first_head, q, zeros), jnp.where(first_head, zeros, q))
    row = lax.broadcasted_iota(jnp.int32, (t, t), 0)
    col = lax.broadcasted_iota(jnp.int32, (t, t), 1)
    strictly_before = col < row
    later_key = jnp.where(row > col, 1.0, 0.0).astype(BF16)

    def scores(head, kblk):
        z = lax.dot_general(q_heads[head], kblk, (((1,), (1,)), ((), ())),
                            preferred_element_type=F32)
        sp = jnp.maximum(z, 0.0) + jnp.log(1.0 + jnp.exp(-jnp.abs(z)))
        return z, sp

    def suffix_sums(parts):
        hi = [p.astype(BF16) for p in parts]
        lo = [(p - h.astype(F32)).astype(BF16) for p, h in zip(parts, hi)]
        stacked = jnp.concatenate(hi + lo, axis=0)
        sums = jnp.dot(stacked, later_key, preferred_element_type=F32)
        m = len(parts)
        return [sums[c * t:(c + 1) * t] + sums[(m + c) * t:(m + c + 1) * t] for c in range(m)]

    def diagonal(head, kblk, vblk):
        z, sp = scores(head, kblk)
        spm = jnp.where(strictly_before, sp, 0.0)
        (cs,) = suffix_sums([spm])
        a = jnp.where(strictly_before, jnp.exp(z - sp - cs), 0.0)
        acc_ref[head] = jnp.dot(a.astype(BF16), vblk, preferred_element_type=F32)
        surv_ref[head] = jnp.sum(spm, axis=1, keepdims=True)

    def diagonal_and_previous(head, kblk, vblk):
        z, sp = scores(head, kblk)
        spm = jnp.where(strictly_before, sp[:, t:], 0.0)
        cs_p, cs_d = suffix_sums([sp[:, :t], spm])
        surv_p = jnp.sum(spm, axis=1, keepdims=True)
        e = z - sp
        a = jnp.concatenate([jnp.exp(e[:, :t] - cs_p - surv_p),
                             jnp.where(strictly_before, jnp.exp(e[:, t:] - cs_d), 0.0)], axis=1)
        acc_ref[head] = jnp.dot(a.astype(BF16), vblk, preferred_element_type=F32)
        surv_ref[head] = surv_p + jnp.sum(sp[:, :t], axis=1, keepdims=True)

    def single(head, kblk, vblk):
        z, sp = scores(head, kblk)
        (cs,) = suffix_sums([sp])
        surv = surv_ref[head]
        a = jnp.exp(z - sp - cs - surv)
        acc_ref[head] += jnp.dot(a.astype(BF16), vblk, preferred_element_type=F32)
        surv_ref[head] = surv + jnp.sum(sp, axis=1, keepdims=True)

    def keep_going():
        least = jnp.minimum(jnp.min(surv_ref[0]), jnp.min(surv_ref[1]))
        return (least < SB_UNDERFLOW).astype(jnp.int32)

    @pl.when(i == 0)
    def _():
        for head in range(2):
            diagonal(head, k_ref[pl.ds(0, t), :], v_ref[pl.ds(0, t), :])

    @pl.when(i > 0)
    def _():
        start = pl.multiple_of((i - 1) * t, t)
        kb = k_ref[pl.ds(start, 2 * t), :]
        vb = v_ref[pl.ds(start, 2 * t), :]
        for head in range(2):
            diagonal_and_previous(head, kb, vb)

    def cond(carry):
        step, go = carry
        return jnp.logical_and(step < i - 1, go > 0)

    def body(carry):
        step, _ = carry
        s = pl.multiple_of((i - 2 - step) * t, t)
        kb = k_ref[pl.ds(s, t), :]
        vb = v_ref[pl.ds(s, t), :]
        for head in range(2):
            single(head, kb, vb)
        return step + 1, keep_going()

    lax.while_loop(cond, body, (jnp.int32(0), keep_going()))
    o_ref[...] = jnp.where(first_head, acc_ref[0], acc_ref[1]).astype(o_ref.dtype)


def _sb_attention(qkv16, batch, seq):
    t = SB_TILE
    d = D_MODEL
    pairs = d // LANES
    qkv = qkv16.reshape(batch, seq, 3 * d)
    out = pl.pallas_call(
        _sb_kernel,
        grid=(batch, pairs, seq // t),
        in_specs=[pl.BlockSpec((None, t, LANES), lambda b, p, i: (b, i, p)),
                  pl.BlockSpec((None, seq, LANES), lambda b, p, i: (b, 0, pairs + p)),
                  pl.BlockSpec((None, seq, LANES), lambda b, p, i: (b, 0, 2 * pairs + p))],
        out_specs=pl.BlockSpec((None, t, LANES), lambda b, p, i: (b, i, p)),
        out_shape=jax.ShapeDtypeStruct((batch, seq, d), BF16),
        scratch_shapes=[pltpu.VMEM((2, t, LANES), F32), pltpu.VMEM((2, t, 1), F32)],
        compiler_params=_params("arbitrary", "arbitrary", "arbitrary"),
        name="sb_attention",
    )(qkv, qkv, qkv)
    return out.reshape(batch * seq, d)


def _conv_kernel(gb_ref, gc_ref, hh_ref, pc_ref, ph_ref, cw_ref, w_ref, h_ref, g_ref, b_ref,
                 o32_ref, o16_ref, w16_ref, *, blocks_per_seq):
    i = pl.program_id(0)

    @pl.when(i == 0)
    def _():
        w16_ref[...] = w_ref[...].astype(BF16)

    u = gc_ref[...] * hh_ref[...]
    halo = pc_ref[...] * ph_ref[...]
    halo = jnp.where(i % blocks_per_seq == 0, jnp.zeros_like(halo), halo)
    prev1 = halo[SUBLANES - 1:SUBLANES, :]
    prev2 = halo[SUBLANES - 2:SUBLANES - 1, :]
    row = lax.broadcasted_iota(jnp.int32, u.shape, 0)
    u1 = jnp.where(row == 0, prev1, pltpu.roll(u, 1, 0))
    u2 = jnp.where(row == 0, prev2, jnp.where(row == 1, prev1, pltpu.roll(u, 2, 0)))
    conv = cw_ref[0:1, :] * u2 + cw_ref[1:2, :] * u1 + cw_ref[2:3, :] * u
    gated = (gb_ref[...] * conv).astype(BF16)
    mix = jnp.dot(gated, w16_ref[...], preferred_element_type=F32)
    out = _deepnorm_layer_norm(h_ref[...], mix, g_ref[...], b_ref[...])
    o32_ref[...] = out
    o16_ref[...] = out.astype(BF16)


def _conv_outproj_ln(proj32, conv_w, w16, h32, g, b, seq):
    n = proj32.shape[0]
    d = D_MODEL
    tm = min(ROW_TILE, seq)
    halo_blocks = tm // SUBLANES
    kern = functools.partial(_conv_kernel, blocks_per_seq=seq // tm)
    return pl.pallas_call(
        kern,
        grid=(n // tm,),
        in_specs=[pl.BlockSpec((tm, d), lambda i: (i, 0)),
                  pl.BlockSpec((tm, d), lambda i: (i, 1)),
                  pl.BlockSpec((tm, d), lambda i: (i, 2)),
                  pl.BlockSpec((SUBLANES, d), lambda i: (jnp.maximum(i * halo_blocks - 1, 0), 1)),
                  pl.BlockSpec((SUBLANES, d), lambda i: (jnp.maximum(i * halo_blocks - 1, 0), 2)),
                  pl.BlockSpec((SC_WIDTH, d), lambda i: (0, 0)),
                  pl.BlockSpec((d, d), lambda i: (0, 0)),
                  pl.BlockSpec((tm, d), lambda i: (i, 0)),
                  pl.BlockSpec((1, d), lambda i: (0, 0)),
                  pl.BlockSpec((1, d), lambda i: (0, 0))],
        out_specs=[pl.BlockSpec((tm, d), lambda i: (i, 0)),
                   pl.BlockSpec((tm, d), lambda i: (i, 0))],
        out_shape=[jax.ShapeDtypeStruct((n, d), F32),
                   jax.ShapeDtypeStruct((n, d), BF16)],
        scratch_shapes=[pltpu.VMEM((d, d), BF16)],
        compiler_params=_params("arbitrary"),
        name="conv_outproj_ln",
    )(proj32, proj32, proj32, proj32, proj32, conv_w, w16, h32, g.reshape(1, d), b.reshape(1, d))


def _rms_norm(x, g):
    ms = jnp.mean(x * x, axis=-1, keepdims=True)
    return x * lax.rsqrt(ms + RMS_EPS) * g


def _mla_prep_kernel(x_ref, pos_ref, freq_ref, sign_ref, win_ref, qn_ref, kvn_ref, wuq_ref, wukv_ref,
                     q_ref, k_ref, v_ref):
    lat = jnp.dot(x_ref[...], win_ref[...], preferred_element_type=F32)
    q_lat = lat[:, :Q_LORA]
    kv_lat = lat[:, Q_LORA:Q_LORA + KV_LORA]
    r0 = Q_LORA + KV_LORA
    k_r = lat[:, r0:r0 + QK_ROPE]
    k_r_swapped = lat[:, r0 + QK_ROPE:r0 + 2 * QK_ROPE]
    qn = _rms_norm(q_lat, qn_ref[...]).astype(BF16)
    kvn = _rms_norm(kv_lat, kvn_ref[...]).astype(BF16)
    qq = jnp.dot(qn, wuq_ref[...], preferred_element_type=F32)
    kv = jnp.dot(kvn, wukv_ref[...], preferred_element_type=F32)
    ang = pos_ref[...].astype(F32) * freq_ref[...]
    cc = jnp.cos(ang)
    ss = jnp.sin(ang) * sign_ref[...]
    k_rope = (k_r * cc + k_r_swapped * ss).astype(BF16)
    pad = jnp.zeros((x_ref.shape[0], MLA_QK_PAD - QK_NOPE - QK_ROPE), BF16)
    for h in range(MLA_HEADS):
        c = h * MLA_QK_PAD
        q_nope = qq[:, c:c + QK_NOPE]
        q_r = qq[:, c + QK_NOPE:c + QK_NOPE + QK_ROPE]
        q_r_swapped = qq[:, c + QK_NOPE + QK_ROPE:c + MLA_QK_PAD]
        q_rope = q_r * cc + q_r_swapped * ss
        q_ref[h, :, 0:QK_NOPE] = (q_nope * (MLA_SCALE * LOG2E)).astype(BF16)
        q_ref[h, :, QK_NOPE:QK_NOPE + QK_ROPE] = (q_rope * (MLA_SCALE * LOG2E)).astype(BF16)
        q_ref[h, :, QK_NOPE + QK_ROPE:MLA_QK_PAD] = pad
        c2 = h * (QK_NOPE + V_HEAD)
        k_ref[h, :, 0:QK_NOPE] = kv[:, c2:c2 + QK_NOPE].astype(BF16)
        k_ref[h, :, QK_NOPE:QK_NOPE + QK_ROPE] = k_rope
        k_ref[h, :, QK_NOPE + QK_ROPE:MLA_QK_PAD] = pad
        v_ref[h] = kv[:, c2 + QK_NOPE:c2 + QK_NOPE + V_HEAD].astype(BF16)


def _swap_halves(w):
    half = w.shape[-1] // 2
    return jnp.concatenate([w[..., half:], w[..., :half]], axis=-1)


def _mla_prep(h16, positions, w_in, q_norm, w_uq, kv_norm, w_ukv, batch, seq):
    n = h16.shape[0]
    d = D_MODEL
    tm = min(ROW_TILE, seq)
    r0 = Q_LORA + KV_LORA
    win = jnp.concatenate([w_in, _swap_halves(w_in[:, r0:r0 + QK_ROPE])], axis=1).astype(BF16)
    wq = w_uq.reshape(Q_LORA, MLA_HEADS, QK_NOPE + QK_ROPE)
    wq = jnp.concatenate([wq, _swap_halves(wq[:, :, QK_NOPE:])], axis=2)
    wq = wq.reshape(Q_LORA, MLA_HEADS * MLA_QK_PAD).astype(BF16)
    wkv = w_ukv.astype(BF16)
    inv_freq = ROPE_THETA ** (-jnp.arange(0, QK_ROPE, 2, dtype=F32) / QK_ROPE)
    freq = jnp.concatenate([inv_freq, inv_freq]).reshape(1, QK_ROPE)
    sign = jnp.concatenate([-jnp.ones((QK_ROPE // 2,), F32),
                            jnp.ones((QK_ROPE // 2,), F32)]).reshape(1, QK_ROPE)
    bps = seq // tm
    head_spec = lambda width: pl.BlockSpec((None, MLA_HEADS, tm, width),
                                           lambda i: (i // bps, 0, i % bps, 0))
    const = lambda shape: pl.BlockSpec(shape, lambda i: (0,) * len(shape))
    return pl.pallas_call(
        _mla_prep_kernel,
        grid=(n // tm,),
        in_specs=[pl.BlockSpec((tm, d), lambda i: (i, 0)),
                  pl.BlockSpec((tm, 1), lambda i: (i, 0)),
                  const((1, QK_ROPE)), const((1, QK_ROPE)),
                  const(win.shape), const((1, Q_LORA)), const((1, KV_LORA)),
                  const(wq.shape), const(wkv.shape)],
        out_specs=[head_spec(MLA_QK_PAD), head_spec(MLA_QK_PAD), head_spec(V_HEAD)],
        out_shape=[jax.ShapeDtypeStruct((batch, MLA_HEADS, seq, MLA_QK_PAD), BF16),
                   jax.ShapeDtypeStruct((batch, MLA_HEADS, seq, MLA_QK_PAD), BF16),
                   jax.ShapeDtypeStruct((batch, MLA_HEADS, seq, V_HEAD), BF16)],
        compiler_params=_params("arbitrary"),
        name="mla_prep",
    )(h16, positions.reshape(n, 1), freq, sign, win, q_norm.reshape(1, Q_LORA),
      kv_norm.reshape(1, KV_LORA), wq, wkv)


def _mla_attn_kernel(q_ref, k_ref, v_ref, o_ref, m_ref, l_ref, acc_ref):
    t = MLA_TILE
    i = pl.program_id(2)
    m_ref[...] = jnp.full(m_ref.shape, -jnp.inf, F32)
    l_ref[...] = jnp.zeros(l_ref.shape, F32)
    acc_ref[...] = jnp.zeros(acc_ref.shape, F32)

    def step(head, start, width, diagonal):
        kb = k_ref[head, pl.ds(start, width), :]
        vb = v_ref[head, pl.ds(start, width), :]
        s = lax.dot_general(q_ref[head], kb, (((1,), (1,)), ((), ())),
                            preferred_element_type=F32)
        if diagonal:
            row = lax.broadcasted_iota(jnp.int32, (t, width), 0)
            col = lax.broadcasted_iota(jnp.int32, (t, width), 1)
            s = jnp.where(col <= row, s, -jnp.inf)
        m_prev = m_ref[head]
        m_new = jnp.maximum(m_prev, jnp.max(s, axis=1, keepdims=True))
        scale = jnp.exp2(m_prev - m_new)
        p = jnp.exp2(s - m_new)
        l_ref[head] = scale * l_ref[head] + jnp.sum(p, axis=1, keepdims=True)
        acc_ref[head] = scale * acc_ref[head] + jnp.dot(p.astype(BF16), vb,
                                                        preferred_element_type=F32)
        m_ref[head] = m_new

    def body(j, carry):
        for head in range(MLA_HEADS_PER_STEP):
            step(head, pl.multiple_of(j * 2 * t, 2 * t), 2 * t, False)
        return carry

    lax.fori_loop(0, i // 2, body, 0)

    @pl.when(i % 2 == 1)
    def _():
        for head in range(MLA_HEADS_PER_STEP):
            step(head, pl.multiple_of((i - 1) * t, t), t, False)

    for head in range(MLA_HEADS_PER_STEP):
        step(head, pl.multiple_of(i * t, t), t, True)
        o_ref[:, head * V_HEAD:(head + 1) * V_HEAD] = (
            acc_ref[head] / l_ref[head]).astype(o_ref.dtype)


def _mla_attention(q, k, v):
    batch, heads, seq, _ = q.shape
    t = MLA_TILE
    hp = MLA_HEADS_PER_STEP
    out = pl.pallas_call(
        _mla_attn_kernel,
        grid=(batch, heads // hp, seq // t),
        in_specs=[pl.BlockSpec((None, hp, t, MLA_QK_PAD), lambda b, h, i: (b, h, i, 0)),
                  pl.BlockSpec((None, hp, seq, MLA_QK_PAD), lambda b, h, i: (b, h, 0, 0)),
                  pl.BlockSpec((None, hp, seq, V_HEAD), lambda b, h, i: (b, h, 0, 0))],
        out_specs=pl.BlockSpec((None, t, hp * V_HEAD), lambda b, h, i: (b, i, h)),
        out_shape=jax.ShapeDtypeStruct((batch, seq, heads * V_HEAD), BF16),
        scratch_shapes=[pltpu.VMEM((hp, t, 1), F32), pltpu.VMEM((hp, t, 1), F32),
                        pltpu.VMEM((hp, t, V_HEAD), F32)],
        compiler_params=_params("arbitrary", "arbitrary", "arbitrary"),
        name="mla_attention",
    )(q, k, v)
    return out.reshape(batch * seq, heads * V_HEAD)


def _router_kernel(h_ref, whi_ref, wlo_ref, b_ref, idx_ref, gate_ref, rank_ref, count_ref, seen_ref):
    @pl.when(pl.program_id(0) == 0)
    def _():
        seen_ref[...] = jnp.zeros(seen_ref.shape, F32)

    x = h_ref[...]
    xhi = x.astype(BF16)
    xlo = (x - xhi.astype(F32)).astype(BF16)
    logits = (jnp.dot(xhi, whi_ref[...], preferred_element_type=F32)
              + jnp.dot(xhi, wlo_ref[...], preferred_element_type=F32)
              + jnp.dot(xlo, whi_ref[...], preferred_element_type=F32)) + b_ref[...]
    lane = lax.broadcasted_iota(jnp.int32, logits.shape, 1)
    lane_f = lane.astype(F32)
    vals = jnp.where(lane < N_EXPERTS, logits, -jnp.inf)
    idx_out = jnp.zeros(logits.shape, jnp.int32)
    gate_out = jnp.zeros(logits.shape, F32)
    picks = []
    top = None
    denom = None
    for k in range(TOP_K):
        m = jnp.max(vals, axis=1, keepdims=True)
        pick = jnp.min(jnp.where(vals == m, lane_f, float(LANES)), axis=1, keepdims=True)
        pick_i = pick.astype(jnp.int32)
        picks.append(pick_i)
        if k == 0:
            top = m
            e = jnp.ones_like(m)
            denom = e
        else:
            e = jnp.exp(m - top)
            denom = denom + e
        idx_out = jnp.where(lane == k, pick_i, idx_out)
        gate_out = jnp.where(lane == k, e, gate_out)
        vals = jnp.where(lane == pick_i, -jnp.inf, vals)
    idx_ref[...] = idx_out
    gate_ref[...] = gate_out / denom

    chosen = jnp.zeros(logits.shape, F32)
    for pick_i in picks:
        chosen = jnp.where(lane == pick_i, 1.0, chosen)
    tm = logits.shape[0]
    earlier = (lax.broadcasted_iota(jnp.int32, (tm, tm), 1)
               < lax.broadcasted_iota(jnp.int32, (tm, tm), 0))
    before = jnp.dot(jnp.where(earlier, 1.0, 0.0).astype(BF16), chosen.astype(BF16),
                     preferred_element_type=F32) + seen_ref[...]
    rank_out = jnp.zeros(logits.shape, jnp.int32)
    for k, pick_i in enumerate(picks):
        rank_k = jnp.sum(jnp.where(lane == pick_i, before, 0.0), axis=1, keepdims=True)
        rank_out = jnp.where(lane == k, rank_k.astype(jnp.int32), rank_out)
    rank_ref[...] = rank_out
    seen = seen_ref[...] + jnp.sum(chosen, axis=0, keepdims=True)
    seen_ref[...] = seen
    count_ref[...] = seen.astype(jnp.int32)


def _router(h32, router_w, router_b):
    n, d = h32.shape
    tm = min(ROW_TILE, n)
    w = jnp.zeros((d, LANES), F32).at[:, :N_EXPERTS].set(router_w)
    whi = w.astype(BF16)
    wlo = (w - whi.astype(F32)).astype(BF16)
    b = jnp.zeros((1, LANES), F32).at[0, :N_EXPERTS].set(router_b)
    tok = lambda: pl.BlockSpec((tm, LANES), lambda i: (i, 0))
    return pl.pallas_call(
        _router_kernel,
        grid=(n // tm,),
        in_specs=[pl.BlockSpec((tm, d), lambda i: (i, 0)),
                  pl.BlockSpec((d, LANES), lambda i: (0, 0)),
                  pl.BlockSpec((d, LANES), lambda i: (0, 0)),
                  pl.BlockSpec((1, LANES), lambda i: (0, 0))],
        out_specs=[tok(), tok(), tok(), pl.BlockSpec((1, LANES), lambda i: (0, 0))],
        out_shape=[jax.ShapeDtypeStruct((n, LANES), jnp.int32),
                   jax.ShapeDtypeStruct((n, LANES), F32),
                   jax.ShapeDtypeStruct((n, LANES), jnp.int32),
                   jax.ShapeDtypeStruct((1, LANES), jnp.int32)],
        scratch_shapes=[pltpu.VMEM((1, LANES), F32)],
        compiler_params=_params("arbitrary"),
        name="router",
    )(h32, whi, wlo, b)


def _routing_tables(top_idx, rank, counts):
    blk = MOE_ROWS
    n = top_idx.shape[0]
    n_blocks = (n * TOP_K + N_EXPERTS * blk) // blk
    padded = (counts + blk - 1) // blk * blk
    padded_end = jnp.cumsum(padded)
    padded_start = padded_end - padded
    experts = jnp.arange(N_EXPERTS, dtype=jnp.int32)
    start_of = jnp.sum(jnp.where(top_idx[..., None] == experts, padded_start, 0), axis=-1)
    dest = (start_of + rank).reshape(-1).astype(jnp.int32)
    block_row = jnp.arange(n_blocks, dtype=jnp.int32)[:, None] * blk
    block_expert = jnp.minimum(jnp.sum((padded_end[None, :] <= block_row).astype(jnp.int32), axis=1),
                               N_EXPERTS - 1).astype(jnp.int32)
    blocks_used = (padded_end[-1] // blk).astype(jnp.int32).reshape(1)
    return dest, block_expert, blocks_used


def _dispatch_kernel(dest_ref, h_ref, xs_in_hbm, xs_hbm, sem_ref):
    del xs_in_hbm
    tb = DISPATCH_ROWS
    i = pl.program_id(0)

    def issue(t, carry):
        for k in range(TOP_K):
            row = dest_ref[(i * tb + t) * TOP_K + k]
            pltpu.make_async_copy(h_ref.at[pl.ds(t, 1), :], xs_hbm.at[pl.ds(row, 1), :],
                                  sem_ref.at[0]).start()
        return carry

    lax.fori_loop(0, tb, issue, 0, unroll=2)
    for k in range(TOP_K):
        pltpu.make_async_copy(h_ref, xs_hbm.at[pl.ds(0, tb), :], sem_ref.at[0]).wait()


def _dispatch(h32, dest, n_rows):
    n, d = h32.shape
    tb = DISPATCH_ROWS
    grid_spec = pltpu.PrefetchScalarGridSpec(
        num_scalar_prefetch=1,
        grid=(n // tb,),
        in_specs=[pl.BlockSpec((tb, d), lambda i, dest: (i, 0)),
                  pl.BlockSpec(memory_space=pl.ANY)],
        out_specs=pl.BlockSpec(memory_space=pl.ANY),
        scratch_shapes=[pltpu.SemaphoreType.DMA((1,))],
    )
    return pl.pallas_call(
        _dispatch_kernel,
        grid_spec=grid_spec,
        out_shape=jax.ShapeDtypeStruct((n_rows, d), F32),
        input_output_aliases={2: 0},
        compiler_params=_params("arbitrary"),
        name="dispatch",
    )(dest, h32, jnp.zeros((n_rows, d), F32))


def _moe_kernel(be_ref, nb_ref, x_ref, wg_ref, bg_ref, wu_ref, bu_ref, wd_ref, bd_ref, y_ref,
                wg16_ref, wu16_ref, wd16_ref):
    b = pl.program_id(0)
    used = nb_ref[0]
    new_expert = jnp.logical_or(b == 0, be_ref[b] != be_ref[jnp.maximum(b - 1, 0)])

    @pl.when(jnp.logical_and(b < used, new_expert))
    def _():
        wg16_ref[...] = wg_ref[...].astype(BF16)
        wu16_ref[...] = wu_ref[...].astype(BF16)
        wd16_ref[...] = wd_ref[...].astype(BF16)

    @pl.when(b < used)
    def _():
        xb = x_ref[...].astype(BF16)
        g = jnp.minimum(jnp.dot(xb, wg16_ref[...], preferred_element_type=F32) + bg_ref[...],
                        SWIGLU_LIMIT)
        u = jnp.clip(jnp.dot(xb, wu16_ref[...], preferred_element_type=F32) + bu_ref[...],
                     -SWIGLU_LIMIT, SWIGLU_LIMIT)
        hdn = g * jax.nn.sigmoid(SWIGLU_ALPHA * g) * (u + 1.0)
        y_ref[...] = jnp.dot(hdn.astype(BF16), wd16_ref[...], preferred_element_type=F32) + bd_ref[...]

    @pl.when(b >= used)
    def _():
        y_ref[...] = jnp.zeros(y_ref.shape, F32)


def _moe_experts(xs, block_expert, blocks_used, layer, w_gate, b_gate, w_up, b_up, w_down, b_down):
    n_rows, d = xs.shape
    tm = MOE_ROWS
    de = w_gate.shape[3]
    expert = lambda rows, cols: pl.BlockSpec((None, None, rows, cols),
                                             lambda b, be, nb: (layer, be[b], 0, 0))
    grid_spec = pltpu.PrefetchScalarGridSpec(
        num_scalar_prefetch=2,
        grid=(n_rows // tm,),
        in_specs=[pl.BlockSpec((tm, d), lambda b, be, nb: (jnp.minimum(b, nb[0] - 1), 0)),
                  expert(d, de), expert(1, de), expert(d, de), expert(1, de),
                  expert(de, d), expert(1, d)],
        out_specs=pl.BlockSpec((tm, d), lambda b, be, nb: (b, 0)),
        scratch_shapes=[pltpu.VMEM((d, de), BF16), pltpu.VMEM((d, de), BF16),
                        pltpu.VMEM((de, d), BF16)],
    )
    return pl.pallas_call(
        _moe_kernel,
        grid_spec=grid_spec,
        out_shape=jax.ShapeDtypeStruct((n_rows, d), F32),
        compiler_params=_params("arbitrary"),
        name="moe_experts",
    )(block_expert, blocks_used, xs, w_gate, b_gate.reshape(DEPTH, N_EXPERTS, 1, de),
      w_up, b_up.reshape(DEPTH, N_EXPERTS, 1, de), w_down, b_down.reshape(DEPTH, N_EXPERTS, 1, d))


def _combine_kernel(dest_ref, y_hbm, gate_ref, h_ref, g_ref, b_ref, o32_ref, o16_ref,
                    buf_ref, sem_ref):
    tb = COMBINE_ROWS
    i = pl.program_id(0)
    steps = pl.num_programs(0)

    def start_gather(block, slot):
        def issue(t, carry):
            for k in range(TOP_K):
                row = dest_ref[(block * tb + t) * TOP_K + k]
                pltpu.make_async_copy(y_hbm.at[pl.ds(row, 1), :],
                                      buf_ref.at[slot, k, pl.ds(t, 1), :],
                                      sem_ref.at[slot]).start()
            return carry
        lax.fori_loop(0, tb, issue, 0, unroll=2)

    def wait_gather(slot):
        for k in range(TOP_K):
            pltpu.make_async_copy(y_hbm.at[pl.ds(0, tb), :], buf_ref.at[slot, k],
                                  sem_ref.at[slot]).wait()

    slot = i % 2

    @pl.when(i == 0)
    def _():
        start_gather(0, 0)

    @pl.when(i + 1 < steps)
    def _():
        start_gather(i + 1, 1 - slot)

    wait_gather(slot)
    gates = gate_ref[...]
    ffn = buf_ref[slot, 0] * gates[:, 0:1]
    for k in range(1, TOP_K):
        ffn = ffn + buf_ref[slot, k] * gates[:, k:k + 1]
    out = _deepnorm_layer_norm(h_ref[...], ffn, g_ref[...], b_ref[...])
    o32_ref[...] = out
    o16_ref[...] = out.astype(BF16)


def _combine_ln(y, dest, gates, h32, g, b):
    n, d = h32.shape
    tb = COMBINE_ROWS
    grid_spec = pltpu.PrefetchScalarGridSpec(
        num_scalar_prefetch=1,
        grid=(n // tb,),
        in_specs=[pl.BlockSpec(memory_space=pl.ANY),
                  pl.BlockSpec((tb, LANES), lambda i, dest: (i, 0)),
                  pl.BlockSpec((tb, d), lambda i, dest: (i, 0)),
                  pl.BlockSpec((1, d), lambda i, dest: (0, 0)),
                  pl.BlockSpec((1, d), lambda i, dest: (0, 0))],
        out_specs=[pl.BlockSpec((tb, d), lambda i, dest: (i, 0)),
                   pl.BlockSpec((tb, d), lambda i, dest: (i, 0))],
        scratch_shapes=[pltpu.VMEM((2, TOP_K, tb, d), F32), pltpu.SemaphoreType.DMA((2,))],
    )
    return pl.pallas_call(
        _combine_kernel,
        grid_spec=grid_spec,
        out_shape=[jax.ShapeDtypeStruct((n, d), F32), jax.ShapeDtypeStruct((n, d), BF16)],
        compiler_params=_params("arbitrary"),
        name="combine_ln",
    )(dest, y, gates, h32, g.reshape(1, d), b.reshape(1, d))


def _routed_ffn_ln(h32, layer, router_w, router_b, w_gate, b_gate, w_up, b_up, w_down, b_down, g, b):
    n = h32.shape[0]
    idx_pad, gate_pad, rank_pad, counts = _router(h32, router_w, router_b)
    dest, block_expert, blocks_used = _routing_tables(idx_pad[:, :TOP_K], rank_pad[:, :TOP_K],
                                                      counts[0, :N_EXPERTS])
    xs = _dispatch(h32, dest, n * TOP_K + N_EXPERTS * MOE_ROWS)
    y = _moe_experts(xs, block_expert, blocks_used, layer,
                     w_gate, b_gate, w_up, b_up, w_down, b_down)
    return _combine_ln(y, dest, gate_pad, h32, g, b)


def kernel(x, positions, ln_g, ln_b, sb_w_in, sb_w_out, sc_w_in, sc_conv_w, sc_w_out, mla_w_in, mla_q_norm, mla_w_uq, mla_kv_norm, mla_w_ukv, mla_w_out, router_w, router_b, moe_w_gate, moe_b_gate, moe_w_up, moe_b_up, moe_w_down, moe_b_down):
    batch, seq, d = x.shape
    n = batch * seq
    h32 = x.reshape(n, d)
    h16 = h32.astype(BF16)
    for layer in range(DEPTH):
        kind, slot = layer % N_MIXERS, layer // N_MIXERS
        g0, b0 = ln_g[layer, 0], ln_b[layer, 0]
        if kind == 0:
            qkv = _proj(h16, sb_w_in[slot], BF16, tn=d)
            attn = _sb_attention(qkv, batch, seq)
            h32, h16 = _outproj_ln(attn, sb_w_out[slot], h32, g0, b0)
        elif kind == 1:
            proj = _proj(h16, sc_w_in[slot], F32, tn=d)
            h32, h16 = _conv_outproj_ln(proj, sc_conv_w[slot], sc_w_out[slot], h32, g0, b0, seq)
        else:
            q, k, v = _mla_prep(h16, positions, mla_w_in[slot], mla_q_norm[slot], mla_w_uq[slot],
                                mla_kv_norm[slot], mla_w_ukv[slot], batch, seq)
            attn = _mla_attention(q, k, v)
            h32, h16 = _outproj_ln(attn, mla_w_out[slot], h32, g0, b0)
        h32, h16 = _routed_ffn_ln(h32, layer, router_w[layer], router_b[layer], moe_w_gate,
                                  moe_b_gate, moe_w_up, moe_b_up, moe_w_down, moe_b_down,
                                  ln_g[layer, 1], ln_b[layer, 1])
    return h32.reshape(batch, seq, d)
```

```python
import functools
import math

import jax
import jax.numpy as jnp
from jax import lax
from jax.experimental import pallas as pl
from jax.experimental.pallas import tpu as pltpu

F32 = jnp.float32
BF16 = jnp.bfloat16

D_MODEL = 1024
DEPTH = 4
N_MIXERS = 3
SB_HEADS = 16
SB_HEAD_DIM = D_MODEL // SB_HEADS
SB_SCALE = 1.0 / math.sqrt(SB_HEAD_DIM)
SC_WIDTH = 3
MLA_HEADS = 8
QK_NOPE = 128
QK_ROPE = 64
V_HEAD = 128
Q_LORA = 384
KV_LORA = 256
MLA_SCALE = 1.0 / math.sqrt(QK_NOPE + QK_ROPE)
ROPE_THETA = 10000.0
N_EXPERTS = 32
TOP_K = 4
SWIGLU_LIMIT = 7.0
SWIGLU_ALPHA = 1.702
LN_EPS = 1e-5
RMS_EPS = 1e-6
DEEPNORM_ALPHA = (2 * DEPTH) ** 0.25
LOG2E = math.log2(math.e)
SB_UNDERFLOW = 110.0

LANES = 128
SUBLANES = 8
VMEM_LIMIT_BYTES = 56 * 1024 * 1024

ROW_TILE = 512
SB_TILE = 256
SB_HEADS_PER_STEP = 4
MLA_TILE = 512
MLA_HEADS_PER_STEP = 2
MOE_ROWS = 256
COMBINE_ROWS = 128
DISPATCH_ROWS = 256
MLA_QK_PAD = 256


def _params(*semantics):
    return pltpu.CompilerParams(dimension_semantics=semantics,
                                vmem_limit_bytes=VMEM_LIMIT_BYTES)


def _deepnorm_layer_norm(h, sub, g, b):
    y = DEEPNORM_ALPHA * h + sub
    mu = jnp.mean(y, axis=-1, keepdims=True)
    d = y - mu
    var = jnp.mean(d * d, axis=-1, keepdims=True)
    return d * lax.rsqrt(var + LN_EPS) * g + b


def _proj_kernel(x_ref, w_ref, o_ref, w16_ref):
    @pl.when(pl.program_id(1) == 0)
    def _():
        w16_ref[...] = w_ref[...].astype(BF16)

    o_ref[...] = jnp.dot(x_ref[...], w16_ref[...],
                         preferred_element_type=F32).astype(o_ref.dtype)


def _proj(x16, w32, out_dtype, tn):
    n, k = x16.shape
    nout = w32.shape[1]
    tm = min(ROW_TILE, n)
    return pl.pallas_call(
        _proj_kernel,
        grid=(nout // tn, n // tm),
        in_specs=[pl.BlockSpec((tm, k), lambda j, i: (i, 0)),
                  pl.BlockSpec((k, tn), lambda j, i: (0, j))],
        out_specs=pl.BlockSpec((tm, tn), lambda j, i: (i, j)),
        out_shape=jax.ShapeDtypeStruct((n, nout), out_dtype),
        scratch_shapes=[pltpu.VMEM((k, tn), BF16)],
        compiler_params=_params("arbitrary", "arbitrary"),
        name="proj",
    )(x16, w32)


def _outproj_ln_kernel(a_ref, w_ref, h_ref, g_ref, b_ref, o32_ref, o16_ref, w16_ref):
    @pl.when(pl.program_id(0) == 0)
    def _():
        w16_ref[...] = w_ref[...].astype(BF16)

    mix = jnp.dot(a_ref[...], w16_ref[...], preferred_element_type=F32)
    out = _deepnorm_layer_norm(h_ref[...], mix, g_ref[...], b_ref[...])
    o32_ref[...] = out
    o16_ref[...] = out.astype(BF16)


def _outproj_ln(a16, w32, h32, g, b):
    n, k = a16.shape
    d = w32.shape[1]
    tm = min(ROW_TILE, n)
    return pl.pallas_call(
        _outproj_ln_kernel,
        grid=(n // tm,),
        in_specs=[pl.BlockSpec((tm, k), lambda i: (i, 0)),
                  pl.BlockSpec((k, d), lambda i: (0, 0)),
                  pl.BlockSpec((tm, d), lambda i: (i, 0)),
                  pl.BlockSpec((1, d), lambda i: (0, 0)),
                  pl.BlockSpec((1, d), lambda i: (0, 0))],
        out_specs=[pl.BlockSpec((tm, d), lambda i: (i, 0)),
                   pl.BlockSpec((tm, d), lambda i: (i, 0))],
        out_shape=[jax.ShapeDtypeStruct((n, d), F32),
                   jax.ShapeDtypeStruct((n, d), BF16)],
        scratch_shapes=[pltpu.VMEM((k, d), BF16)],
        compiler_params=_params("arbitrary"),
        name="outproj_ln",
    )(a16, w32, h32, g.reshape(1, d), b.reshape(1, d))


def _sb_kernel(q_ref, k_ref, v_ref, o_ref, acc_ref, surv_ref):
    t = SB_TILE
    i = pl.program_id(2)
    lane = lax.broadcasted_iota(jnp.int32, (t, LANES), 1)
    first_head = lane < SB_HEAD_DIM
    heads = range(SB_HEADS_PER_STEP)
    pair_lanes = lambda head: slice((head // 2) * LANES, (head // 2 + 1) * LANES)
    q_heads = []
    for head in heads:
        q = q_ref[:, pair_lanes(head)] * SB_SCALE
        keep = first_head if head % 2 == 0 else jnp.logical_not(first_head)
        q_heads.append(jnp.where(keep, q, jnp.zeros_like(q)))
    row = lax.broadcasted_iota(jnp.int32, (t, t), 0)
    col = lax.broadcasted_iota(jnp.int32, (t, t), 1)
    strictly_before = col < row
    later_key = jnp.where(row > col, 1.0, 0.0).astype(BF16)

    def scores(head, kblk):
        z = lax.dot_general(q_heads[head], kblk, (((1,), (1,)), ((), ())),
                            preferred_element_type=F32)
        sp = jnp.maximum(z, 0.0) + jnp.log(1.0 + jnp.exp(-jnp.abs(z)))
        return z, sp

    def suffix_sums(parts):
        hi = [p.astype(BF16) for p in parts]
        lo = [(p - h.astype(F32)).astype(BF16) for p, h in zip(parts, hi)]
        stacked = jnp.concatenate(hi + lo, axis=0)
        sums = jnp.dot(stacked, later_key, preferred_element_type=F32)
        m = len(parts)
        return [sums[c * t:(c + 1) * t] + sums[(m + c) * t:(m + c + 1) * t] for c in range(m)]

    def diagonal(head, kblk, vblk):
        z, sp = scores(head, kblk)
        spm = jnp.where(strictly_before, sp, 0.0)
        (cs,) = suffix_sums([spm])
        a = jnp.where(strictly_before, jnp.exp(z - sp - cs), 0.0)
        acc_ref[head] = jnp.dot(a.astype(BF16), vblk, preferred_element_type=F32)
        surv_ref[head] = jnp.sum(spm, axis=1, keepdims=True)

    def diagonal_and_previous(head, kblk, vblk):
        z, sp = scores(head, kblk)
        spm = jnp.where(strictly_before, sp[:, t:], 0.0)
        cs_p, cs_d = suffix_sums([sp[:, :t], spm])
        surv_p = jnp.sum(spm, axis=1, keepdims=True)
        e = z - sp
        a = jnp.concatenate([jnp.exp(e[:, :t] - cs_p - surv_p),
                             jnp.where(strictly_before, jnp.exp(e[:, t:] - cs_d), 0.0)], axis=1)
        acc_ref[head] = jnp.dot(a.astype(BF16), vblk, preferred_element_type=F32)
        surv_ref[head] = surv_p + jnp.sum(sp[:, :t], axis=1, keepdims=True)

    def single(head, kblk, vblk):
        z, sp = scores(head, kblk)
        (cs,) = suffix_sums([sp])
        surv = surv_ref[head]
        a = jnp.exp(z - sp - cs - surv)
        acc_ref[head] += jnp.dot(a.astype(BF16), vblk, preferred_element_type=F32)
        surv_ref[head] = surv + jnp.sum(sp, axis=1, keepdims=True)

    def keep_going():
        least = jnp.min(surv_ref[0])
        for head in heads[1:]:
            least = jnp.minimum(least, jnp.min(surv_ref[head]))
        return (least < SB_UNDERFLOW).astype(jnp.int32)

    def sweep(tile_fn, start, rows):
        for head in heads:
            tile_fn(head, k_ref[pl.ds(start, rows), pair_lanes(head)],
                    v_ref[pl.ds(start, rows), pair_lanes(head)])

    @pl.when(i == 0)
    def _():
        sweep(diagonal, 0, t)

    @pl.when(i > 0)
    def _():
        sweep(diagonal_and_previous, pl.multiple_of((i - 1) * t, t), 2 * t)

    def cond(carry):
        step, go = carry
        return jnp.logical_and(step < i - 1, go > 0)

    def body(carry):
        step, _ = carry
        sweep(single, pl.multiple_of((i - 2 - step) * t, t), t)
        return step + 1, keep_going()

    lax.while_loop(cond, body, (jnp.int32(0), keep_going()))
    for head in heads[::2]:
        o_ref[:, pair_lanes(head)] = jnp.where(first_head, acc_ref[head],
                                               acc_ref[head + 1]).astype(o_ref.dtype)


def _sb_attention(qkv16, batch, seq):
    t = SB_TILE
    d = D_MODEL
    hs = SB_HEADS_PER_STEP
    width = hs * SB_HEAD_DIM
    groups = d // width
    qkv = qkv16.reshape(batch, seq, 3 * d)
    out = pl.pallas_call(
        _sb_kernel,
        grid=(batch, groups, seq // t),
        in_specs=[pl.BlockSpec((None, t, width), lambda b, p, i: (b, i, p)),
                  pl.BlockSpec((None, seq, width), lambda b, p, i: (b, 0, groups + p)),
                  pl.BlockSpec((None, seq, width), lambda b, p, i: (b, 0, 2 * groups + p))],
        out_specs=pl.BlockSpec((None, t, width), lambda b, p, i: (b, i, p)),
        out_shape=jax.ShapeDtypeStruct((batch, seq, d), BF16),
        scratch_shapes=[pltpu.VMEM((hs, t, LANES), F32), pltpu.VMEM((hs, t, 1), F32)],
        compiler_params=_params("arbitrary", "arbitrary", "arbitrary"),
        name="sb_attention",
    )(qkv, qkv, qkv)
    return out.reshape(batch * seq, d)


def _conv_kernel(gb_ref, gc_ref, hh_ref, pc_ref, ph_ref, cw_ref, w_ref, h_ref, g_ref, b_ref,
                 o32_ref, o16_ref, w16_ref, *, blocks_per_seq):
    i = pl.program_id(0)

    @pl.when(i == 0)
    def _():
        w16_ref[...] = w_ref[...].astype(BF16)

    u = gc_ref[...] * hh_ref[...]
    halo = pc_ref[...] * ph_ref[...]
    halo = jnp.where(i % blocks_per_seq == 0, jnp.zeros_like(halo), halo)
    prev1 = halo[SUBLANES - 1:SUBLANES, :]
    prev2 = halo[SUBLANES - 2:SUBLANES - 1, :]
    row = lax.broadcasted_iota(jnp.int32, u.shape, 0)
    u1 = jnp.where(row == 0, prev1, pltpu.roll(u, 1, 0))
    u2 = jnp.where(row == 0, prev2, jnp.where(row == 1, prev1, pltpu.roll(u, 2, 0)))
    conv = cw_ref[0:1, :] * u2 + cw_ref[1:2, :] * u1 + cw_ref[2:3, :] * u
    gated = (gb_ref[...] * conv).astype(BF16)
    mix = jnp.dot(gated, w16_ref[...], preferred_element_type=F32)
    out = _deepnorm_layer_norm(h_ref[...], mix, g_ref[...], b_ref[...])
    o32_ref[...] = out
    o16_ref[...] = out.astype(BF16)


def _conv_outproj_ln(proj32, conv_w, w16, h32, g, b, seq):
    n = proj32.shape[0]
    d = D_MODEL
    tm = min(ROW_TILE, seq)
    halo_blocks = tm // SUBLANES
    kern = functools.partial(_conv_kernel, blocks_per_seq=seq // tm)
    return pl.pallas_call(
        kern,
        grid=(n // tm,),
        in_specs=[pl.BlockSpec((tm, d), lambda i: (i, 0)),
                  pl.BlockSpec((tm, d), lambda i: (i, 1)),
                  pl.BlockSpec((tm, d), lambda i: (i, 2)),
                  pl.BlockSpec((SUBLANES, d), lambda i: (jnp.maximum(i * halo_blocks - 1, 0), 1)),
                  pl.BlockSpec((SUBLANES, d), lambda i: (jnp.maximum(i * halo_blocks - 1, 0), 2)),
                  pl.BlockSpec((SC_WIDTH, d), lambda i: (0, 0)),
                  pl.BlockSpec((d, d), lambda i: (0, 0)),
                  pl.BlockSpec((tm, d), lambda i: (i, 0)),
                  pl.BlockSpec((1, d), lambda i: (0, 0)),
                  pl.BlockSpec((1, d), lambda i: (0, 0))],
        out_specs=[pl.BlockSpec((tm, d), lambda i: (i, 0)),
                   pl.BlockSpec((tm, d), lambda i: (i, 0))],
        out_shape=[jax.ShapeDtypeStruct((n, d), F32),
                   jax.ShapeDtypeStruct((n, d), BF16)],
        scratch_shapes=[pltpu.VMEM((d, d), BF16)],
        compiler_params=_params("arbitrary"),
        name="conv_outproj_ln",
    )(proj32, proj32, proj32, proj32, proj32, conv_w, w16, h32, g.reshape(1, d), b.reshape(1, d))


def _rms_norm(x, g):
    ms = jnp.mean(x * x, axis=-1, keepdims=True)
    return x * lax.rsqrt(ms + RMS_EPS) * g


def _mla_prep_kernel(x_ref, pos_ref, freq_ref, sign_ref, win_ref, qn_ref, kvn_ref, wuq_ref, wukv_ref,
                     q_ref, k_ref, v_ref):
    lat = jnp.dot(x_ref[...], win_ref[...], preferred_element_type=F32)
    q_lat = lat[:, :Q_LORA]
    kv_lat = lat[:, Q_LORA:Q_LORA + KV_LORA]
    r0 = Q_LORA + KV_LORA
    k_r = lat[:, r0:r0 + QK_ROPE]
    k_r_swapped = lat[:, r0 + QK_ROPE:r0 + 2 * QK_ROPE]
    qn = _rms_norm(q_lat, qn_ref[...]).astype(BF16)
    kvn = _rms_norm(kv_lat, kvn_ref[...]).astype(BF16)
    qq = jnp.dot(qn, wuq_ref[...], preferred_element_type=F32)
    kv = jnp.dot(kvn, wukv_ref[...], preferred_element_type=F32)
    ang = pos_ref[...].astype(F32) * freq_ref[...]
    cc = jnp.cos(ang)
    ss = jnp.sin(ang) * sign_ref[...]
    k_rope = (k_r * cc + k_r_swapped * ss).astype(BF16)
    pad = jnp.zeros((x_ref.shape[0], MLA_QK_PAD - QK_NOPE - QK_ROPE), BF16)
    for h in range(MLA_HEADS):
        c = h * MLA_QK_PAD
        q_nope = qq[:, c:c + QK_NOPE]
        q_r = qq[:, c + QK_NOPE:c + QK_NOPE + QK_ROPE]
        q_r_swapped = qq[:, c + QK_NOPE + QK_ROPE:c + MLA_QK_PAD]
        q_rope = q_r * cc + q_r_swapped * ss
        q_ref[h, :, 0:QK_NOPE] = (q_nope * (MLA_SCALE * LOG2E)).astype(BF16)
        q_ref[h, :, QK_NOPE:QK_NOPE + QK_ROPE] = (q_rope * (MLA_SCALE * LOG2E)).astype(BF16)
        q_ref[h, :, QK_NOPE + QK_ROPE:MLA_QK_PAD] = pad
        c2 = h * (QK_NOPE + V_HEAD)
        k_ref[h, :, 0:QK_NOPE] = kv[:, c2:c2 + QK_NOPE].astype(BF16)
        k_ref[h, :, QK_NOPE:QK_NOPE + QK_ROPE] = k_rope
        k_ref[h, :, QK_NOPE + QK_ROPE:MLA_QK_PAD] = pad
        v_ref[h] = kv[:, c2 + QK_NOPE:c2 + QK_NOPE + V_HEAD].astype(BF16)


def _swap_halves(w):
    half = w.shape[-1] // 2
    return jnp.concatenate([w[..., half:], w[..., :half]], axis=-1)


def _mla_prep(h16, positions, w_in, q_norm, w_uq, kv_norm, w_ukv, batch, seq):
    n = h16.shape[0]
    d = D_MODEL
    tm = min(ROW_TILE, seq)
    r0 = Q_LORA + KV_LORA
    win = jnp.concatenate([w_in, _swap_halves(w_in[:, r0:r0 + QK_ROPE])], axis=1).astype(BF16)
    wq = w_uq.reshape(Q_LORA, MLA_HEADS, QK_NOPE + QK_ROPE)
    wq = jnp.concatenate([wq, _swap_halves(wq[:, :, QK_NOPE:])], axis=2)
    wq = wq.reshape(Q_LORA, MLA_HEADS * MLA_QK_PAD).astype(BF16)
    wkv = w_ukv.astype(BF16)
    inv_freq = ROPE_THETA ** (-jnp.arange(0, QK_ROPE, 2, dtype=F32) / QK_ROPE)
    freq = jnp.concatenate([inv_freq, inv_freq]).reshape(1, QK_ROPE)
    sign = jnp.concatenate([-jnp.ones((QK_ROPE // 2,), F32),
                            jnp.ones((QK_ROPE // 2,), F32)]).reshape(1, QK_ROPE)
    bps = seq // tm
    head_spec = lambda width: pl.BlockSpec((None, MLA_HEADS, tm, width),
                                           lambda i: (i // bps, 0, i % bps, 0))
    const = lambda shape: pl.BlockSpec(shape, lambda i: (0,) * len(shape))
    return pl.pallas_call(
        _mla_prep_kernel,
        grid=(n // tm,),
        in_specs=[pl.BlockSpec((tm, d), lambda i: (i, 0)),
                  pl.BlockSpec((tm, 1), lambda i: (i, 0)),
                  const((1, QK_ROPE)), const((1, QK_ROPE)),
                  const(win.shape), const((1, Q_LORA)), const((1, KV_LORA)),
                  const(wq.shape), const(wkv.shape)],
        out_specs=[head_spec(MLA_QK_PAD), head_spec(MLA_QK_PAD), head_spec(V_HEAD)],
        out_shape=[jax.ShapeDtypeStruct((batch, MLA_HEADS, seq, MLA_QK_PAD), BF16),
                   jax.ShapeDtypeStruct((batch, MLA_HEADS, seq, MLA_QK_PAD), BF16),
                   jax.ShapeDtypeStruct((batch, MLA_HEADS, seq, V_HEAD), BF16)],
        compiler_params=_params("arbitrary"),
        name="mla_prep",
    )(h16, positions.reshape(n, 1), freq, sign, win, q_norm.reshape(1, Q_LORA),
      kv_norm.reshape(1, KV_LORA), wq, wkv)


def _mla_attn_kernel(q_ref, k_ref, v_ref, o_ref, m_ref, l_ref, acc_ref, s_ref, p_ref, scale_ref):
    t = MLA_TILE
    i = pl.program_id(2)
    heads = range(MLA_HEADS_PER_STEP)
    m_ref[...] = jnp.full(m_ref.shape, -jnp.inf, F32)
    l_ref[...] = jnp.zeros(l_ref.shape, F32)
    acc_ref[...] = jnp.zeros(acc_ref.shape, F32)
    p_ref[1] = jnp.zeros(p_ref.shape[1:], BF16)
    scale_ref[1] = jnp.ones(scale_ref.shape[1:], F32)

    def scores(j, slot):
        start = pl.multiple_of(j * t, t)
        for head in heads:
            s_ref[slot, head] = lax.dot_general(
                q_ref[head], k_ref[head, pl.ds(start, t), :], (((1,), (1,)), ((), ())),
                preferred_element_type=F32)

    def accumulate(j, slot):
        start = pl.multiple_of(j * t, t)
        for head in heads:
            pv = jnp.dot(p_ref[slot, head], v_ref[head, pl.ds(start, t), :],
                         preferred_element_type=F32)
            acc_ref[head] = scale_ref[slot, head] * acc_ref[head] + pv

    def softmax(slot, diagonal):
        for head in heads:
            s = s_ref[slot, head]
            if diagonal:
                row = lax.broadcasted_iota(jnp.int32, (t, t), 0)
                col = lax.broadcasted_iota(jnp.int32, (t, t), 1)
                s = jnp.where(col <= row, s, -jnp.inf)
            m_prev = m_ref[head]
            m_new = jnp.maximum(m_prev, jnp.max(s, axis=1, keepdims=True))
            scale = jnp.exp2(m_prev - m_new)
            p = jnp.exp2(s - m_new)
            l_ref[head] = scale * l_ref[head] + jnp.sum(p, axis=1, keepdims=True)
            m_ref[head] = m_new
            scale_ref[slot, head] = scale
            p_ref[slot, head] = p.astype(BF16)

    scores(0, 0)

    def body(jj, carry):
        j = 2 * jj
        scores(j + 1, 1)
        accumulate(jnp.maximum(j - 1, 0), 1)
        softmax(0, False)
        scores(j + 2, 0)
        accumulate(j, 0)
        softmax(1, False)
        return carry

    lax.fori_loop(0, i // 2, body, 0)

    @pl.when(i % 2 == 0)
    def _():
        accumulate(jnp.maximum(i - 1, 0), 1)
        softmax(0, True)
        accumulate(i, 0)

    @pl.when(i % 2 == 1)
    def _():
        scores(i, 1)
        accumulate(jnp.maximum(i - 2, 0), 1)
        softmax(0, False)
        accumulate(i - 1, 0)
        softmax(1, True)
        accumulate(i, 1)

    for head in heads:
        o_ref[:, head * V_HEAD:(head + 1) * V_HEAD] = (
            acc_ref[head] / l_ref[head]).astype(o_ref.dtype)


def _mla_attention(q, k, v):
    batch, heads, seq, _ = q.shape
    t = MLA_TILE
    hp = MLA_HEADS_PER_STEP
    out = pl.pallas_call(
        _mla_attn_kernel,
        grid=(batch, heads // hp, seq // t),
        in_specs=[pl.BlockSpec((None, hp, t, MLA_QK_PAD), lambda b, h, i: (b, h, i, 0)),
                  pl.BlockSpec((None, hp, seq, MLA_QK_PAD), lambda b, h, i: (b, h, 0, 0)),
                  pl.BlockSpec((None, hp, seq, V_HEAD), lambda b, h, i: (b, h, 0, 0))],
        out_specs=pl.BlockSpec((None, t, hp * V_HEAD), lambda b, h, i: (b, i, h)),
        out_shape=jax.ShapeDtypeStruct((batch, seq, heads * V_HEAD), BF16),
        scratch_shapes=[pltpu.VMEM((hp, t, 1), F32), pltpu.VMEM((hp, t, 1), F32),
                        pltpu.VMEM((hp, t, V_HEAD), F32),
                        pltpu.VMEM((2, hp, t, t), F32), pltpu.VMEM((2, hp, t, t), BF16),
                        pltpu.VMEM((2, hp, t, 1), F32)],
        compiler_params=_params("arbitrary", "arbitrary", "arbitrary"),
        name="mla_attention",
    )(q, k, v)
    return out.reshape(batch * seq, heads * V_HEAD)


def _router_kernel(h_ref, whi_ref, wlo_ref, b_ref, idx_ref, gate_ref, rank_ref, count_ref, seen_ref):
    @pl.when(pl.program_id(0) == 0)
    def _():
        seen_ref[...] = jnp.zeros(seen_ref.shape, F32)

    x = h_ref[...]
    xhi = x.astype(BF16)
    xlo = (x - xhi.astype(F32)).astype(BF16)
    logits = (jnp.dot(xhi, whi_ref[...], preferred_element_type=F32)
              + jnp.dot(xhi, wlo_ref[...], preferred_element_type=F32)
              + jnp.dot(xlo, whi_ref[...], preferred_element_type=F32)) + b_ref[...]
    lane = lax.broadcasted_iota(jnp.int32, logits.shape, 1)
    lane_f = lane.astype(F32)
    vals = jnp.where(lane < N_EXPERTS, logits, -jnp.inf)
    idx_out = jnp.zeros(logits.shape, jnp.int32)
    gate_out = jnp.zeros(logits.shape, F32)
    picks = []
    top = None
    denom = None
    for k in range(TOP_K):
        m = jnp.max(vals, axis=1, keepdims=True)
        pick = jnp.min(jnp.where(vals == m, lane_f, float(LANES)), axis=1, keepdims=True)
        pick_i = pick.astype(jnp.int32)
        picks.append(pick_i)
        if k == 0:
            top = m
            e = jnp.ones_like(m)
            denom = e
        else:
            e = jnp.exp(m - top)
            denom = denom + e
        idx_out = jnp.where(lane == k, pick_i, idx_out)
        gate_out = jnp.where(lane == k, e, gate_out)
        vals = jnp.where(lane == pick_i, -jnp.inf, vals)
    idx_ref[...] = idx_out
    gate_ref[...] = gate_out / denom

    chosen = jnp.zeros(logits.shape, F32)
    for pick_i in picks:
        chosen = jnp.where(lane == pick_i, 1.0, chosen)
    tm = logits.shape[0]
    earlier = (lax.broadcasted_iota(jnp.int32, (tm, tm), 1)
               < lax.broadcasted_iota(jnp.int32, (tm, tm), 0))
    before = jnp.dot(jnp.where(earlier, 1.0, 0.0).astype(BF16), chosen.astype(BF16),
                     preferred_element_type=F32) + seen_ref[...]
    rank_out = jnp.zeros(logits.shape, jnp.int32)
    for k, pick_i in enumerate(picks):
        rank_k = jnp.sum(jnp.where(lane == pick_i, before, 0.0), axis=1, keepdims=True)
        rank_out = jnp.where(lane == k, rank_k.astype(jnp.int32), rank_out)
    rank_ref[...] = rank_out
    seen = seen_ref[...] + jnp.sum(chosen, axis=0, keepdims=True)
    seen_ref[...] = seen
    count_ref[...] = seen.astype(jnp.int32)


def _router(h32, router_w, router_b):
    n, d = h32.shape
    tm = min(ROW_TILE, n)
    w = jnp.zeros((d, LANES), F32).at[:, :N_EXPERTS].set(router_w)
    whi = w.astype(BF16)
    wlo = (w - whi.astype(F32)).astype(BF16)
    b = jnp.zeros((1, LANES), F32).at[0, :N_EXPERTS].set(router_b)
    tok = lambda: pl.BlockSpec((tm, LANES), lambda i: (i, 0))
    return pl.pallas_call(
        _router_kernel,
        grid=(n // tm,),
        in_specs=[pl.BlockSpec((tm, d), lambda i: (i, 0)),
                  pl.BlockSpec((d, LANES), lambda i: (0, 0)),
                  pl.BlockSpec((d, LANES), lambda i: (0, 0)),
                  pl.BlockSpec((1, LANES), lambda i: (0, 0))],
        out_specs=[tok(), tok(), tok(), pl.BlockSpec((1, LANES), lambda i: (0, 0))],
        out_shape=[jax.ShapeDtypeStruct((n, LANES), jnp.int32),
                   jax.ShapeDtypeStruct((n, LANES), F32),
                   jax.ShapeDtypeStruct((n, LANES), jnp.int32),
                   jax.ShapeDtypeStruct((1, LANES), jnp.int32)],
        scratch_shapes=[pltpu.VMEM((1, LANES), F32)],
        compiler_params=_params("arbitrary"),
        name="router",
    )(h32, whi, wlo, b)


def _routing_tables(top_idx, rank, counts):
    blk = MOE_ROWS
    n = top_idx.shape[0]
    n_blocks = (n * TOP_K + N_EXPERTS * blk) // blk
    padded = (counts + blk - 1) // blk * blk
    padded_end = jnp.cumsum(padded)
    padded_start = padded_end - padded
    experts = jnp.arange(N_EXPERTS, dtype=jnp.int32)
    start_of = jnp.sum(jnp.where(top_idx[..., None] == experts, padded_start, 0), axis=-1)
    dest = (start_of + rank).reshape(-1).astype(jnp.int32)
    block_row = jnp.arange(n_blocks, dtype=jnp.int32)[:, None] * blk
    block_expert = jnp.minimum(jnp.sum((padded_end[None, :] <= block_row).astype(jnp.int32), axis=1),
                               N_EXPERTS - 1).astype(jnp.int32)
    blocks_used = (padded_end[-1] // blk).astype(jnp.int32).reshape(1)
    return dest, block_expert, blocks_used


def _dispatch_kernel(dest_ref, h_ref, xs_in_hbm, xs_hbm, sem_ref):
    del xs_in_hbm
    tb = DISPATCH_ROWS
    i = pl.program_id(0)

    def issue(t, carry):
        for k in range(TOP_K):
            row = dest_ref[(i * tb + t) * TOP_K + k]
            pltpu.make_async_copy(h_ref.at[pl.ds(t, 1), :], xs_hbm.at[pl.ds(row, 1), :],
                                  sem_ref.at[0]).start()
        return carry

    lax.fori_loop(0, tb, issue, 0, unroll=2)
    for k in range(TOP_K):
        pltpu.make_async_copy(h_ref, xs_hbm.at[pl.ds(0, tb), :], sem_ref.at[0]).wait()


def _dispatch(h32, dest, n_rows):
    n, d = h32.shape
    tb = DISPATCH_ROWS
    grid_spec = pltpu.PrefetchScalarGridSpec(
        num_scalar_prefetch=1,
        grid=(n // tb,),
        in_specs=[pl.BlockSpec((tb, d), lambda i, dest: (i, 0)),
                  pl.BlockSpec(memory_space=pl.ANY)],
        out_specs=pl.BlockSpec(memory_space=pl.ANY),
        scratch_shapes=[pltpu.SemaphoreType.DMA((1,))],
    )
    return pl.pallas_call(
        _dispatch_kernel,
        grid_spec=grid_spec,
        out_shape=jax.ShapeDtypeStruct((n_rows, d), F32),
        input_output_aliases={2: 0},
        compiler_params=_params("arbitrary"),
        name="dispatch",
    )(dest, h32, jnp.zeros((n_rows, d), F32))


def _moe_kernel(be_ref, nb_ref, x_ref, wg_ref, bg_ref, wu_ref, bu_ref, wd_ref, bd_ref, y_ref,
                wg16_ref, wu16_ref, wd16_ref):
    b = pl.program_id(0)
    used = nb_ref[0]
    new_expert = jnp.logical_or(b == 0, be_ref[b] != be_ref[jnp.maximum(b - 1, 0)])

    @pl.when(jnp.logical_and(b < used, new_expert))
    def _():
        wg16_ref[...] = wg_ref[...].astype(BF16)
        wu16_ref[...] = wu_ref[...].astype(BF16)
        wd16_ref[...] = wd_ref[...].astype(BF16)

    @pl.when(b < used)
    def _():
        xb = x_ref[...].astype(BF16)
        g = jnp.minimum(jnp.dot(xb, wg16_ref[...], preferred_element_type=F32) + bg_ref[...],
                        SWIGLU_LIMIT)
        u = jnp.clip(jnp.dot(xb, wu16_ref[...], preferred_element_type=F32) + bu_ref[...],
                     -SWIGLU_LIMIT, SWIGLU_LIMIT)
        hdn = g * jax.nn.sigmoid(SWIGLU_ALPHA * g) * (u + 1.0)
        y_ref[...] = jnp.dot(hdn.astype(BF16), wd16_ref[...], preferred_element_type=F32) + bd_ref[...]

    @pl.when(b >= used)
    def _():
        y_ref[...] = jnp.zeros(y_ref.shape, F32)


def _moe_experts(xs, block_expert, blocks_used, layer, w_gate, b_gate, w_up, b_up, w_down, b_down):
    n_rows, d = xs.shape
    tm = MOE_ROWS
    de = w_gate.shape[3]
    expert = lambda rows, cols: pl.BlockSpec((None, None, rows, cols),
                                             lambda b, be, nb: (layer, be[b], 0, 0))
    grid_spec = pltpu.PrefetchScalarGridSpec(
        num_scalar_prefetch=2,
        grid=(n_rows // tm,),
        in_specs=[pl.BlockSpec((tm, d), lambda b, be, nb: (jnp.minimum(b, nb[0] - 1), 0)),
                  expert(d, de), expert(1, de), expert(d, de), expert(1, de),
                  expert(de, d), expert(1, d)],
        out_specs=pl.BlockSpec((tm, d), lambda b, be, nb: (b, 0)),
        scratch_shapes=[pltpu.VMEM((d, de), BF16), pltpu.VMEM((d, de), BF16),
                        pltpu.VMEM((de, d), BF16)],
    )
    return pl.pallas_call(
        _moe_kernel,
        grid_spec=grid_spec,
        out_shape=jax.ShapeDtypeStruct((n_rows, d), F32),
        compiler_params=_params("arbitrary"),
        name="moe_experts",
    )(block_expert, blocks_used, xs, w_gate, b_gate.reshape(DEPTH, N_EXPERTS, 1, de),
      w_up, b_up.reshape(DEPTH, N_EXPERTS, 1, de), w_down, b_down.reshape(DEPTH, N_EXPERTS, 1, d))


def _combine_kernel(dest_ref, y_hbm, gate_ref, h_ref, g_ref, b_ref, o32_ref, o16_ref,
                    buf_ref, sem_ref):
    tb = COMBINE_ROWS
    i = pl.program_id(0)
    steps = pl.num_programs(0)

    def start_gather(block, slot):
        def issue(t, carry):
            for k in range(TOP_K):
                row = dest_ref[(block * tb + t) * TOP_K + k]
                pltpu.make_async_copy(y_hbm.at[pl.ds(row, 1), :],
                                      buf_ref.at[slot, k, pl.ds(t, 1), :],
                                      sem_ref.at[slot]).start()
            return carry
        lax.fori_loop(0, tb, issue, 0, unroll=2)

    def wait_gather(slot):
        for k in range(TOP_K):
            pltpu.make_async_copy(y_hbm.at[pl.ds(0, tb), :], buf_ref.at[slot, k],
                                  sem_ref.at[slot]).wait()

    slot = i % 2

    @pl.when(i == 0)
    def _():
        start_gather(0, 0)

    @pl.when(i + 1 < steps)
    def _():
        start_gather(i + 1, 1 - slot)

    wait_gather(slot)
    gates = gate_ref[...]
    ffn = buf_ref[slot, 0] * gates[:, 0:1]
    for k in range(1, TOP_K):
        ffn = ffn + buf_ref[slot, k] * gates[:, k:k + 1]
    out = _deepnorm_layer_norm(h_ref[...], ffn, g_ref[...], b_ref[...])
    o32_ref[...] = out
    o16_ref[...] = out.astype(BF16)


def _combine_ln(y, dest, gates, h32, g, b):
    n, d = h32.shape
    tb = COMBINE_ROWS
    grid_spec = pltpu.PrefetchScalarGridSpec(
        num_scalar_prefetch=1,
        grid=(n // tb,),
        in_specs=[pl.BlockSpec(memory_space=pl.ANY),
                  pl.BlockSpec((tb, LANES), lambda i, dest: (i, 0)),
                  pl.BlockSpec((tb, d), lambda i, dest: (i, 0)),
                  pl.BlockSpec((1, d), lambda i, dest: (0, 0)),
                  pl.BlockSpec((1, d), lambda i, dest: (0, 0))],
        out_specs=[pl.BlockSpec((tb, d), lambda i, dest: (i, 0)),
                   pl.BlockSpec((tb, d), lambda i, dest: (i, 0))],
        scratch_shapes=[pltpu.VMEM((2, TOP_K, tb, d), F32), pltpu.SemaphoreType.DMA((2,))],
    )
    return pl.pallas_call(
        _combine_kernel,
        grid_spec=grid_spec,
        out_shape=[jax.ShapeDtypeStruct((n, d), F32), jax.ShapeDtypeStruct((n, d), BF16)],
        compiler_params=_params("arbitrary"),
        name="combine_ln",
    )(dest, y, gates, h32, g.reshape(1, d), b.reshape(1, d))


def _routed_ffn_ln(h32, layer, router_w, router_b, w_gate, b_gate, w_up, b_up, w_down, b_down, g, b):
    n = h32.shape[0]
    idx_pad, gate_pad, rank_pad, counts = _router(h32, router_w, router_b)
    dest, block_expert, blocks_used = _routing_tables(idx_pad[:, :TOP_K], rank_pad[:, :TOP_K],
                                                      counts[0, :N_EXPERTS])
    xs = _dispatch(h32, dest, n * TOP_K + N_EXPERTS * MOE_ROWS)
    y = _moe_experts(xs, block_expert, blocks_used, layer,
                     w_gate, b_gate, w_up, b_up, w_down, b_down)
    return _combine_ln(y, dest, gate_pad, h32, g, b)


def kernel(x, positions, ln_g, ln_b, sb_w_in, sb_w_out, sc_w_in, sc_conv_w, sc_w_out, mla_w_in, mla_q_norm, mla_w_uq, mla_kv_norm, mla_w_ukv, mla_w_out, router_w, router_b, moe_w_gate, moe_b_gate, moe_w_up, moe_b_up, moe_w_down, moe_b_down):
    batch, seq, d = x.shape
    n = batch * seq
    h32 = x.reshape(n, d)
    h16 = h32.astype(BF16)
    for layer in range(DEPTH):
        kind, slot = layer % N_MIXERS, layer // N_MIXERS
        g0, b0 = ln_g[layer, 0], ln_b[layer, 0]
        if kind == 0:
            qkv = _proj(h16, sb_w_in[slot], BF16, tn=d)
            attn = _sb_attention(qkv, batch, seq)
            h32, h16 = _outproj_ln(attn, sb_w_out[slot], h32, g0, b0)
        elif kind == 1:
            proj = _proj(h16, sc_w_in[slot], F32, tn=d)
            h32, h16 = _conv_outproj_ln(proj, sc_conv_w[slot], sc_w_out[slot], h32, g0, b0, seq)
        else:
            q, k, v = _mla_prep(h16, positions, mla_w_in[slot], mla_q_norm[slot], mla_w_uq[slot],
                                mla_kv_norm[slot], mla_w_ukv[slot], batch, seq)
            attn = _mla_attention(q, k, v)
            h32, h16 = _outproj_ln(attn, mla_w_out[slot], h32, g0, b0)
        h32, h16 = _routed_ffn_ln(h32, layer, router_w[layer], router_b[layer], moe_w_gate,
                                  moe_b_gate, moe_w_up, moe_b_up, moe_w_down, moe_b_down,
                                  ln_g[layer, 1], ln_b[layer, 1])
    return h32.reshape(batch, seq, d)
```

```python
import functools
import math

import jax
import jax.numpy as jnp
from jax import lax
from jax.experimental import pallas as pl
from jax.experimental.pallas import tpu as pltpu

F32 = jnp.float32
BF16 = jnp.bfloat16

D_MODEL = 1024
DEPTH = 4
N_MIXERS = 3
SB_HEADS = 16
SB_HEAD_DIM = D_MODEL // SB_HEADS
SB_SCALE = 1.0 / math.sqrt(SB_HEAD_DIM)
SC_WIDTH = 3
MLA_HEADS = 8
QK_NOPE = 128
QK_ROPE = 64
V_HEAD = 128
Q_LORA = 384
KV_LORA = 256
MLA_SCALE = 1.0 / math.sqrt(QK_NOPE + QK_ROPE)
ROPE_THETA = 10000.0
N_EXPERTS = 32
TOP_K = 4
SWIGLU_LIMIT = 7.0
SWIGLU_ALPHA = 1.702
LN_EPS = 1e-5
RMS_EPS = 1e-6
DEEPNORM_ALPHA = (2 * DEPTH) ** 0.25
LOG2E = math.log2(math.e)
SB_UNDERFLOW = 110.0

LANES = 128
SUBLANES = 8
VMEM_LIMIT_BYTES = 56 * 1024 * 1024

ROW_TILE = 512
SB_TILE = 256
SB_HEADS_PER_STEP = 4
MLA_TILE = 512
MLA_HEADS_PER_STEP = 2
MOE_ROWS = 256
COMBINE_ROWS = 128
DISPATCH_ROWS = 256
MLA_QK_PAD = 256


def _params(*semantics):
    return pltpu.CompilerParams(dimension_semantics=semantics,
                                vmem_limit_bytes=VMEM_LIMIT_BYTES)


def _deepnorm_layer_norm(h, sub, g, b):
    y = DEEPNORM_ALPHA * h + sub
    mu = jnp.mean(y, axis=-1, keepdims=True)
    d = y - mu
    var = jnp.mean(d * d, axis=-1, keepdims=True)
    return d * lax.rsqrt(var + LN_EPS) * g + b


def _proj_kernel(x_ref, w_ref, o_ref, w16_ref):
    @pl.when(pl.program_id(1) == 0)
    def _():
        w16_ref[...] = w_ref[...].astype(BF16)

    o_ref[...] = jnp.dot(x_ref[...], w16_ref[...],
                         preferred_element_type=F32).astype(o_ref.dtype)


def _proj(x16, w32, out_dtype, tn):
    n, k = x16.shape
    nout = w32.shape[1]
    tm = min(ROW_TILE, n)
    return pl.pallas_call(
        _proj_kernel,
        grid=(nout // tn, n // tm),
        in_specs=[pl.BlockSpec((tm, k), lambda j, i: (i, 0)),
                  pl.BlockSpec((k, tn), lambda j, i: (0, j))],
        out_specs=pl.BlockSpec((tm, tn), lambda j, i: (i, j)),
        out_shape=jax.ShapeDtypeStruct((n, nout), out_dtype),
        scratch_shapes=[pltpu.VMEM((k, tn), BF16)],
        compiler_params=_params("arbitrary", "arbitrary"),
        name="proj",
    )(x16, w32)


def _outproj_ln_kernel(a_ref, w_ref, h_ref, g_ref, b_ref, o32_ref, o16_ref, w16_ref):
    @pl.when(pl.program_id(0) == 0)
    def _():
        w16_ref[...] = w_ref[...].astype(BF16)

    mix = jnp.dot(a_ref[...], w16_ref[...], preferred_element_type=F32)
    out = _deepnorm_layer_norm(h_ref[...], mix, g_ref[...], b_ref[...])
    o32_ref[...] = out
    o16_ref[...] = out.astype(BF16)


def _outproj_ln(a16, w32, h32, g, b):
    n, k = a16.shape
    d = w32.shape[1]
    tm = min(ROW_TILE, n)
    return pl.pallas_call(
        _outproj_ln_kernel,
        grid=(n // tm,),
        in_specs=[pl.BlockSpec((tm, k), lambda i: (i, 0)),
                  pl.BlockSpec((k, d), lambda i: (0, 0)),
                  pl.BlockSpec((tm, d), lambda i: (i, 0)),
                  pl.BlockSpec((1, d), lambda i: (0, 0)),
                  pl.BlockSpec((1, d), lambda i: (0, 0))],
        out_specs=[pl.BlockSpec((tm, d), lambda i: (i, 0)),
                   pl.BlockSpec((tm, d), lambda i: (i, 0))],
        out_shape=[jax.ShapeDtypeStruct((n, d), F32),
                   jax.ShapeDtypeStruct((n, d), BF16)],
        scratch_shapes=[pltpu.VMEM((k, d), BF16)],
        compiler_params=_params("arbitrary"),
        name="outproj_ln",
    )(a16, w32, h32, g.reshape(1, d), b.reshape(1, d))


def _sb_kernel(q_ref, k_ref, v_ref, o_ref, acc_ref, surv_ref):
    t = SB_TILE
    i = pl.program_id(2)
    lane = lax.broadcasted_iota(jnp.int32, (t, LANES), 1)
    first_head = lane < SB_HEAD_DIM
    heads = range(SB_HEADS_PER_STEP)
    pair_lanes = lambda head: slice((head // 2) * LANES, (head // 2 + 1) * LANES)
    q_heads = []
    for head in heads:
        q = q_ref[:, pair_lanes(head)] * SB_SCALE
        keep = first_head if head % 2 == 0 else jnp.logical_not(first_head)
        q_heads.append(jnp.where(keep, q, jnp.zeros_like(q)))
    row = lax.broadcasted_iota(jnp.int32, (t, t), 0)
    col = lax.broadcasted_iota(jnp.int32, (t, t), 1)
    strictly_before = col < row
    later_key = jnp.where(row > col, 1.0, 0.0).astype(BF16)

    def scores(head, kblk):
        z = lax.dot_general(q_heads[head], kblk, (((1,), (1,)), ((), ())),
                            preferred_element_type=F32)
        sp = jnp.maximum(z, 0.0) + jnp.log(1.0 + jnp.exp(-jnp.abs(z)))
        return z, sp

    def suffix_sums(parts):
        stacked = jnp.concatenate([p.astype(BF16) for p in parts], axis=0)
        sums = jnp.dot(stacked, later_key, preferred_element_type=F32)
        return [sums[c * t:(c + 1) * t] for c in range(len(parts))]

    def diagonal(head, kblk, vblk):
        z, sp = scores(head, kblk)
        spm = jnp.where(strictly_before, sp, 0.0)
        (cs,) = suffix_sums([spm])
        a = jnp.where(strictly_before, jnp.exp(z - sp - cs), 0.0)
        acc_ref[head] = jnp.dot(a.astype(BF16), vblk, preferred_element_type=F32)
        surv_ref[head] = jnp.sum(spm, axis=1, keepdims=True)

    def diagonal_and_previous(head, kblk, vblk):
        z, sp = scores(head, kblk)
        spm = jnp.where(strictly_before, sp[:, t:], 0.0)
        cs_p, cs_d = suffix_sums([sp[:, :t], spm])
        surv_p = jnp.sum(spm, axis=1, keepdims=True)
        e = z - sp
        a = jnp.concatenate([jnp.exp(e[:, :t] - cs_p - surv_p),
                             jnp.where(strictly_before, jnp.exp(e[:, t:] - cs_d), 0.0)], axis=1)
        acc_ref[head] = jnp.dot(a.astype(BF16), vblk, preferred_element_type=F32)
        surv_ref[head] = surv_p + jnp.sum(sp[:, :t], axis=1, keepdims=True)

    def single(head, kblk, vblk):
        z, sp = scores(head, kblk)
        (cs,) = suffix_sums([sp])
        surv = surv_ref[head]
        a = jnp.exp(z - sp - cs - surv)
        acc_ref[head] += jnp.dot(a.astype(BF16), vblk, preferred_element_type=F32)
        surv_ref[head] = surv + jnp.sum(sp, axis=1, keepdims=True)

    def keep_going():
        least = jnp.min(surv_ref[0])
        for head in heads[1:]:
            least = jnp.minimum(least, jnp.min(surv_ref[head]))
        return (least < SB_UNDERFLOW).astype(jnp.int32)

    def sweep(tile_fn, start, rows):
        for head in heads:
            tile_fn(head, k_ref[pl.ds(start, rows), pair_lanes(head)],
                    v_ref[pl.ds(start, rows), pair_lanes(head)])

    @pl.when(i == 0)
    def _():
        sweep(diagonal, 0, t)

    @pl.when(i > 0)
    def _():
        sweep(diagonal_and_previous, pl.multiple_of((i - 1) * t, t), 2 * t)

    def cond(carry):
        step, go = carry
        return jnp.logical_and(step < i - 1, go > 0)

    def body(carry):
        step, _ = carry
        sweep(single, pl.multiple_of((i - 2 - step) * t, t), t)
        return step + 1, keep_going()

    lax.while_loop(cond, body, (jnp.int32(0), keep_going()))
    for head in heads[::2]:
        o_ref[:, pair_lanes(head)] = jnp.where(first_head, acc_ref[head],
                                               acc_ref[head + 1]).astype(o_ref.dtype)


def _sb_attention(qkv16, batch, seq):
    t = SB_TILE
    d = D_MODEL
    hs = SB_HEADS_PER_STEP
    width = hs * SB_HEAD_DIM
    groups = d // width
    qkv = qkv16.reshape(batch, seq, 3 * d)
    out = pl.pallas_call(
        _sb_kernel,
        grid=(batch, groups, seq // t),
        in_specs=[pl.BlockSpec((None, t, width), lambda b, p, i: (b, i, p)),
                  pl.BlockSpec((None, seq, width), lambda b, p, i: (b, 0, groups + p)),
                  pl.BlockSpec((None, seq, width), lambda b, p, i: (b, 0, 2 * groups + p))],
        out_specs=pl.BlockSpec((None, t, width), lambda b, p, i: (b, i, p)),
        out_shape=jax.ShapeDtypeStruct((batch, seq, d), BF16),
        scratch_shapes=[pltpu.VMEM((hs, t, LANES), F32), pltpu.VMEM((hs, t, 1), F32)],
        compiler_params=_params("arbitrary", "arbitrary", "arbitrary"),
        name="sb_attention",
    )(qkv, qkv, qkv)
    return out.reshape(batch * seq, d)


def _conv_kernel(gb_ref, gc_ref, hh_ref, pc_ref, ph_ref, cw_ref, w_ref, h_ref, g_ref, b_ref,
                 o32_ref, o16_ref, w16_ref, *, blocks_per_seq):
    i = pl.program_id(0)

    @pl.when(i == 0)
    def _():
        w16_ref[...] = w_ref[...].astype(BF16)

    u = gc_ref[...] * hh_ref[...]
    halo = pc_ref[...] * ph_ref[...]
    halo = jnp.where(i % blocks_per_seq == 0, jnp.zeros_like(halo), halo)
    prev1 = halo[SUBLANES - 1:SUBLANES, :]
    prev2 = halo[SUBLANES - 2:SUBLANES - 1, :]
    row = lax.broadcasted_iota(jnp.int32, u.shape, 0)
    u1 = jnp.where(row == 0, prev1, pltpu.roll(u, 1, 0))
    u2 = jnp.where(row == 0, prev2, jnp.where(row == 1, prev1, pltpu.roll(u, 2, 0)))
    conv = cw_ref[0:1, :] * u2 + cw_ref[1:2, :] * u1 + cw_ref[2:3, :] * u
    gated = (gb_ref[...] * conv).astype(BF16)
    mix = jnp.dot(gated, w16_ref[...], preferred_element_type=F32)
    out = _deepnorm_layer_norm(h_ref[...], mix, g_ref[...], b_ref[...])
    o32_ref[...] = out
    o16_ref[...] = out.astype(BF16)


def _conv_outproj_ln(proj32, conv_w, w16, h32, g, b, seq):
    n = proj32.shape[0]
    d = D_MODEL
    tm = min(ROW_TILE, seq)
    halo_blocks = tm // SUBLANES
    kern = functools.partial(_conv_kernel, blocks_per_seq=seq // tm)
    return pl.pallas_call(
        kern,
        grid=(n // tm,),
        in_specs=[pl.BlockSpec((tm, d), lambda i: (i, 0)),
                  pl.BlockSpec((tm, d), lambda i: (i, 1)),
                  pl.BlockSpec((tm, d), lambda i: (i, 2)),
                  pl.BlockSpec((SUBLANES, d), lambda i: (jnp.maximum(i * halo_blocks - 1, 0), 1)),
                  pl.BlockSpec((SUBLANES, d), lambda i: (jnp.maximum(i * halo_blocks - 1, 0), 2)),
                  pl.BlockSpec((SC_WIDTH, d), lambda i: (0, 0)),
                  pl.BlockSpec((d, d), lambda i: (0, 0)),
                  pl.BlockSpec((tm, d), lambda i: (i, 0)),
                  pl.BlockSpec((1, d), lambda i: (0, 0)),
                  pl.BlockSpec((1, d), lambda i: (0, 0))],
        out_specs=[pl.BlockSpec((tm, d), lambda i: (i, 0)),
                   pl.BlockSpec((tm, d), lambda i: (i, 0))],
        out_shape=[jax.ShapeDtypeStruct((n, d), F32),
                   jax.ShapeDtypeStruct((n, d), BF16)],
        scratch_shapes=[pltpu.VMEM((d, d), BF16)],
        compiler_params=_params("arbitrary"),
        name="conv_outproj_ln",
    )(proj32, proj32, proj32, proj32, proj32, conv_w, w16, h32, g.reshape(1, d), b.reshape(1, d))


def _rms_norm(x, g):
    ms = jnp.mean(x * x, axis=-1, keepdims=True)
    return x * lax.rsqrt(ms + RMS_EPS) * g


def _mla_prep_kernel(x_ref, pos_ref, freq_ref, sign_ref, win_ref, qn_ref, kvn_ref, wuq_ref, wukv_ref,
                     q_ref, k_ref, v_ref):
    lat = jnp.dot(x_ref[...], win_ref[...], preferred_element_type=F32)
    q_lat = lat[:, :Q_LORA]
    kv_lat = lat[:, Q_LORA:Q_LORA + KV_LORA]
    r0 = Q_LORA + KV_LORA
    k_r = lat[:, r0:r0 + QK_ROPE]
    k_r_swapped = lat[:, r0 + QK_ROPE:r0 + 2 * QK_ROPE]
    qn = _rms_norm(q_lat, qn_ref[...]).astype(BF16)
    kvn = _rms_norm(kv_lat, kvn_ref[...]).astype(BF16)
    qq = jnp.dot(qn, wuq_ref[...], preferred_element_type=F32)
    kv = jnp.dot(kvn, wukv_ref[...], preferred_element_type=F32)
    ang = pos_ref[...].astype(F32) * freq_ref[...]
    cc = jnp.cos(ang)
    ss = jnp.sin(ang) * sign_ref[...]
    k_rope = (k_r * cc + k_r_swapped * ss).astype(BF16)
    pad = jnp.zeros((x_ref.shape[0], MLA_QK_PAD - QK_NOPE - QK_ROPE), BF16)
    for h in range(MLA_HEADS):
        c = h * MLA_QK_PAD
        q_nope = qq[:, c:c + QK_NOPE]
        q_r = qq[:, c + QK_NOPE:c + QK_NOPE + QK_ROPE]
        q_r_swapped = qq[:, c + QK_NOPE + QK_ROPE:c + MLA_QK_PAD]
        q_rope = q_r * cc + q_r_swapped * ss
        q_ref[h, :, 0:QK_NOPE] = (q_nope * (MLA_SCALE * LOG2E)).astype(BF16)
        q_ref[h, :, QK_NOPE:QK_NOPE + QK_ROPE] = (q_rope * (MLA_SCALE * LOG2E)).astype(BF16)
        q_ref[h, :, QK_NOPE + QK_ROPE:MLA_QK_PAD] = pad
        c2 = h * (QK_NOPE + V_HEAD)
        k_ref[h, :, 0:QK_NOPE] = kv[:, c2:c2 + QK_NOPE].astype(BF16)
        k_ref[h, :, QK_NOPE:QK_NOPE + QK_ROPE] = k_rope
        k_ref[h, :, QK_NOPE + QK_ROPE:MLA_QK_PAD] = pad
        v_ref[h] = kv[:, c2 + QK_NOPE:c2 + QK_NOPE + V_HEAD].astype(BF16)


def _swap_halves(w):
    half = w.shape[-1] // 2
    return jnp.concatenate([w[..., half:], w[..., :half]], axis=-1)


def _mla_prep(h16, positions, w_in, q_norm, w_uq, kv_norm, w_ukv, batch, seq):
    n = h16.shape[0]
    d = D_MODEL
    tm = min(ROW_TILE, seq)
    r0 = Q_LORA + KV_LORA
    win = jnp.concatenate([w_in, _swap_halves(w_in[:, r0:r0 + QK_ROPE])], axis=1).astype(BF16)
    wq = w_uq.reshape(Q_LORA, MLA_HEADS, QK_NOPE + QK_ROPE)
    wq = jnp.concatenate([wq, _swap_halves(wq[:, :, QK_NOPE:])], axis=2)
    wq = wq.reshape(Q_LORA, MLA_HEADS * MLA_QK_PAD).astype(BF16)
    wkv = w_ukv.astype(BF16)
    inv_freq = ROPE_THETA ** (-jnp.arange(0, QK_ROPE, 2, dtype=F32) / QK_ROPE)
    freq = jnp.concatenate([inv_freq, inv_freq]).reshape(1, QK_ROPE)
    sign = jnp.concatenate([-jnp.ones((QK_ROPE // 2,), F32),
                            jnp.ones((QK_ROPE // 2,), F32)]).reshape(1, QK_ROPE)
    bps = seq // tm
    head_spec = lambda width: pl.BlockSpec((None, MLA_HEADS, tm, width),
                                           lambda i: (i // bps, 0, i % bps, 0))
    const = lambda shape: pl.BlockSpec(shape, lambda i: (0,) * len(shape))
    return pl.pallas_call(
        _mla_prep_kernel,
        grid=(n // tm,),
        in_specs=[pl.BlockSpec((tm, d), lambda i: (i, 0)),
                  pl.BlockSpec((tm, 1), lambda i: (i, 0)),
                  const((1, QK_ROPE)), const((1, QK_ROPE)),
                  const(win.shape), const((1, Q_LORA)), const((1, KV_LORA)),
                  const(wq.shape), const(wkv.shape)],
        out_specs=[head_spec(MLA_QK_PAD), head_spec(MLA_QK_PAD), head_spec(V_HEAD)],
        out_shape=[jax.ShapeDtypeStruct((batch, MLA_HEADS, seq, MLA_QK_PAD), BF16),
                   jax.ShapeDtypeStruct((batch, MLA_HEADS, seq, MLA_QK_PAD), BF16),
                   jax.ShapeDtypeStruct((batch, MLA_HEADS, seq, V_HEAD), BF16)],
        compiler_params=_params("arbitrary"),
        name="mla_prep",
    )(h16, positions.reshape(n, 1), freq, sign, win, q_norm.reshape(1, Q_LORA),
      kv_norm.reshape(1, KV_LORA), wq, wkv)


def _mla_attn_kernel(q_ref, k_ref, v_ref, o_ref, m_ref, l_ref, acc_ref, s_ref, p_ref, scale_ref):
    t = MLA_TILE
    i = pl.program_id(2)
    heads = range(MLA_HEADS_PER_STEP)
    m_ref[...] = jnp.full(m_ref.shape, -jnp.inf, F32)
    l_ref[...] = jnp.zeros(l_ref.shape, F32)
    acc_ref[...] = jnp.zeros(acc_ref.shape, F32)
    p_ref[1] = jnp.zeros(p_ref.shape[1:], BF16)
    scale_ref[1] = jnp.ones(scale_ref.shape[1:], F32)

    def scores(j, slot):
        start = pl.multiple_of(j * t, t)
        for head in heads:
            s_ref[slot, head] = lax.dot_general(
                q_ref[head], k_ref[head, pl.ds(start, t), :], (((1,), (1,)), ((), ())),
                preferred_element_type=F32)

    def accumulate(j, slot):
        start = pl.multiple_of(j * t, t)
        for head in heads:
            pv = jnp.dot(p_ref[slot, head], v_ref[head, pl.ds(start, t), :],
                         preferred_element_type=F32)
            acc_ref[head] = scale_ref[slot, head] * acc_ref[head] + pv

    def softmax(slot, diagonal):
        for head in heads:
            s = s_ref[slot, head]
            if diagonal:
                row = lax.broadcasted_iota(jnp.int32, (t, t), 0)
                col = lax.broadcasted_iota(jnp.int32, (t, t), 1)
                s = jnp.where(col <= row, s, -jnp.inf)
            m_prev = m_ref[head]
            m_new = jnp.maximum(m_prev, jnp.max(s, axis=1, keepdims=True))
            scale = jnp.exp2(m_prev - m_new)
            p = jnp.exp2(s - m_new)
            l_ref[head] = scale * l_ref[head] + jnp.sum(p, axis=1, keepdims=True)
            m_ref[head] = m_new
            scale_ref[slot, head] = scale
            p_ref[slot, head] = p.astype(BF16)

    scores(0, 0)

    def body(jj, carry):
        j = 2 * jj
        scores(j + 1, 1)
        accumulate(jnp.maximum(j - 1, 0), 1)
        softmax(0, False)
        scores(j + 2, 0)
        accumulate(j, 0)
        softmax(1, False)
        return carry

    lax.fori_loop(0, i // 2, body, 0)

    @pl.when(i % 2 == 0)
    def _():
        accumulate(jnp.maximum(i - 1, 0), 1)
        softmax(0, True)
        accumulate(i, 0)

    @pl.when(i % 2 == 1)
    def _():
        scores(i, 1)
        accumulate(jnp.maximum(i - 2, 0), 1)
        softmax(0, False)
        accumulate(i - 1, 0)
        softmax(1, True)
        accumulate(i, 1)

    for head in heads:
        o_ref[:, head * V_HEAD:(head + 1) * V_HEAD] = (
            acc_ref[head] / l_ref[head]).astype(o_ref.dtype)


def _mla_attention(q, k, v):
    batch, heads, seq, _ = q.shape
    t = MLA_TILE
    hp = MLA_HEADS_PER_STEP
    out = pl.pallas_call(
        _mla_attn_kernel,
        grid=(batch, heads // hp, seq // t),
        in_specs=[pl.BlockSpec((None, hp, t, MLA_QK_PAD), lambda b, h, i: (b, h, i, 0)),
                  pl.BlockSpec((None, hp, seq, MLA_QK_PAD), lambda b, h, i: (b, h, 0, 0)),
                  pl.BlockSpec((None, hp, seq, V_HEAD), lambda b, h, i: (b, h, 0, 0))],
        out_specs=pl.BlockSpec((None, t, hp * V_HEAD), lambda b, h, i: (b, i, h)),
        out_shape=jax.ShapeDtypeStruct((batch, seq, heads * V_HEAD), BF16),
        scratch_shapes=[pltpu.VMEM((hp, t, 1), F32), pltpu.VMEM((hp, t, 1), F32),
                        pltpu.VMEM((hp, t, V_HEAD), F32),
                        pltpu.VMEM((2, hp, t, t), F32), pltpu.VMEM((2, hp, t, t), BF16),
                        pltpu.VMEM((2, hp, t, 1), F32)],
        compiler_params=_params("arbitrary", "arbitrary", "arbitrary"),
        name="mla_attention",
    )(q, k, v)
    return out.reshape(batch * seq, heads * V_HEAD)


def _router_kernel(h_ref, whi_ref, wlo_ref, b_ref, idx_ref, gate_ref, rank_ref, count_ref, seen_ref):
    @pl.when(pl.program_id(0) == 0)
    def _():
        seen_ref[...] = jnp.zeros(seen_ref.shape, F32)

    x = h_ref[...]
    xhi = x.astype(BF16)
    xlo = (x - xhi.astype(F32)).astype(BF16)
    logits = (jnp.dot(xhi, whi_ref[...], preferred_element_type=F32)
              + jnp.dot(xhi, wlo_ref[...], preferred_element_type=F32)
              + jnp.dot(xlo, whi_ref[...], preferred_element_type=F32)) + b_ref[...]
    lane = lax.broadcasted_iota(jnp.int32, logits.shape, 1)
    lane_f = lane.astype(F32)
    vals = jnp.where(lane < N_EXPERTS, logits, -jnp.inf)
    idx_out = jnp.zeros(logits.shape, jnp.int32)
    gate_out = jnp.zeros(logits.shape, F32)
    picks = []
    top = None
    denom = None
    for k in range(TOP_K):
        m = jnp.max(vals, axis=1, keepdims=True)
        pick = jnp.min(jnp.where(vals == m, lane_f, float(LANES)), axis=1, keepdims=True)
        pick_i = pick.astype(jnp.int32)
        picks.append(pick_i)
        if k == 0:
            top = m
            e = jnp.ones_like(m)
            denom = e
        else:
            e = jnp.exp(m - top)
            denom = denom + e
        idx_out = jnp.where(lane == k, pick_i, idx_out)
        gate_out = jnp.where(lane == k, e, gate_out)
        vals = jnp.where(lane == pick_i, -jnp.inf, vals)
    idx_ref[...] = idx_out
    gate_ref[...] = gate_out / denom

    chosen = jnp.zeros(logits.shape, F32)
    for pick_i in picks:
        chosen = jnp.where(lane == pick_i, 1.0, chosen)
    tm = logits.shape[0]
    earlier = (lax.broadcasted_iota(jnp.int32, (tm, tm), 1)
               < lax.broadcasted_iota(jnp.int32, (tm, tm), 0))
    before = jnp.dot(jnp.where(earlier, 1.0, 0.0).astype(BF16), chosen.astype(BF16),
                     preferred_element_type=F32) + seen_ref[...]
    rank_out = jnp.zeros(logits.shape, jnp.int32)
    for k, pick_i in enumerate(picks):
        rank_k = jnp.sum(jnp.where(lane == pick_i, before, 0.0), axis=1, keepdims=True)
        rank_out = jnp.where(lane == k, rank_k.astype(jnp.int32), rank_out)
    rank_ref[...] = rank_out
    seen = seen_ref[...] + jnp.sum(chosen, axis=0, keepdims=True)
    seen_ref[...] = seen
    count_ref[...] = seen.astype(jnp.int32)


def _router(h32, router_w, router_b):
    n, d = h32.shape
    tm = min(ROW_TILE, n)
    w = jnp.zeros((d, LANES), F32).at[:, :N_EXPERTS].set(router_w)
    whi = w.astype(BF16)
    wlo = (w - whi.astype(F32)).astype(BF16)
    b = jnp.zeros((1, LANES), F32).at[0, :N_EXPERTS].set(router_b)
    tok = lambda: pl.BlockSpec((tm, LANES), lambda i: (i, 0))
    return pl.pallas_call(
        _router_kernel,
        grid=(n // tm,),
        in_specs=[pl.BlockSpec((tm, d), lambda i: (i, 0)),
                  pl.BlockSpec((d, LANES), lambda i: (0, 0)),
                  pl.BlockSpec((d, LANES), lambda i: (0, 0)),
                  pl.BlockSpec((1, LANES), lambda i: (0, 0))],
        out_specs=[tok(), tok(), tok(), pl.BlockSpec((1, LANES), lambda i: (0, 0))],
        out_shape=[jax.ShapeDtypeStruct((n, LANES), jnp.int32),
                   jax.ShapeDtypeStruct((n, LANES), F32),
                   jax.ShapeDtypeStruct((n, LANES), jnp.int32),
                   jax.ShapeDtypeStruct((1, LANES), jnp.int32)],
        scratch_shapes=[pltpu.VMEM((1, LANES), F32)],
        compiler_params=_params("arbitrary"),
        name="router",
    )(h32, whi, wlo, b)


def _routing_tables(top_idx, rank, counts):
    blk = MOE_ROWS
    n = top_idx.shape[0]
    n_blocks = (n * TOP_K + N_EXPERTS * blk) // blk
    padded = (counts + blk - 1) // blk * blk
    padded_end = jnp.cumsum(padded)
    padded_start = padded_end - padded
    experts = jnp.arange(N_EXPERTS, dtype=jnp.int32)
    start_of = jnp.sum(jnp.where(top_idx[..., None] == experts, padded_start, 0), axis=-1)
    dest = (start_of + rank).reshape(-1).astype(jnp.int32)
    block_row = jnp.arange(n_blocks, dtype=jnp.int32)[:, None] * blk
    block_expert = jnp.minimum(jnp.sum((padded_end[None, :] <= block_row).astype(jnp.int32), axis=1),
                               N_EXPERTS - 1).astype(jnp.int32)
    blocks_used = (padded_end[-1] // blk).astype(jnp.int32).reshape(1)
    return dest, block_expert, blocks_used


def _dispatch_kernel(dest_ref, h_ref, xs_in_hbm, xs_hbm, sem_ref):
    del xs_in_hbm
    tb = DISPATCH_ROWS
    i = pl.program_id(0)

    def issue(t, carry):
        for k in range(TOP_K):
            row = dest_ref[(i * tb + t) * TOP_K + k]
            pltpu.make_async_copy(h_ref.at[pl.ds(t, 1), :], xs_hbm.at[pl.ds(row, 1), :],
                                  sem_ref.at[0]).start()
        return carry

    lax.fori_loop(0, tb, issue, 0, unroll=2)
    for k in range(TOP_K):
        pltpu.make_async_copy(h_ref, xs_hbm.at[pl.ds(0, tb), :], sem_ref.at[0]).wait()


def _dispatch(h32, dest, xs_buffer):
    n, d = h32.shape
    n_rows = xs_buffer.shape[0]
    tb = DISPATCH_ROWS
    grid_spec = pltpu.PrefetchScalarGridSpec(
        num_scalar_prefetch=1,
        grid=(n // tb,),
        in_specs=[pl.BlockSpec((tb, d), lambda i, dest: (i, 0)),
                  pl.BlockSpec(memory_space=pl.ANY)],
        out_specs=pl.BlockSpec(memory_space=pl.ANY),
        scratch_shapes=[pltpu.SemaphoreType.DMA((1,))],
    )
    return pl.pallas_call(
        _dispatch_kernel,
        grid_spec=grid_spec,
        out_shape=jax.ShapeDtypeStruct((n_rows, d), F32),
        input_output_aliases={2: 0},
        compiler_params=_params("arbitrary"),
        name="dispatch",
    )(dest, h32, xs_buffer)


def _moe_kernel(be_ref, nb_ref, x_ref, wg_ref, bg_ref, wu_ref, bu_ref, wd_ref, bd_ref, y_ref,
                wg16_ref, wu16_ref, wd16_ref):
    b = pl.program_id(0)
    used = nb_ref[0]
    new_expert = jnp.logical_or(b == 0, be_ref[b] != be_ref[jnp.maximum(b - 1, 0)])

    @pl.when(jnp.logical_and(b < used, new_expert))
    def _():
        wg16_ref[...] = wg_ref[...].astype(BF16)
        wu16_ref[...] = wu_ref[...].astype(BF16)
        wd16_ref[...] = wd_ref[...].astype(BF16)

    @pl.when(b < used)
    def _():
        xb = x_ref[...].astype(BF16)
        g = jnp.minimum(jnp.dot(xb, wg16_ref[...], preferred_element_type=F32) + bg_ref[...],
                        SWIGLU_LIMIT)
        u = jnp.clip(jnp.dot(xb, wu16_ref[...], preferred_element_type=F32) + bu_ref[...],
                     -SWIGLU_LIMIT, SWIGLU_LIMIT)
        hdn = g * jax.nn.sigmoid(SWIGLU_ALPHA * g) * (u + 1.0)
        y_ref[...] = jnp.dot(hdn.astype(BF16), wd16_ref[...], preferred_element_type=F32) + bd_ref[...]

    @pl.when(b >= used)
    def _():
        y_ref[...] = jnp.zeros(y_ref.shape, F32)


def _moe_experts(xs, block_expert, blocks_used, layer, w_gate, b_gate, w_up, b_up, w_down, b_down):
    n_rows, d = xs.shape
    tm = MOE_ROWS
    de = w_gate.shape[3]
    expert = lambda rows, cols: pl.BlockSpec((None, None, rows, cols),
                                             lambda b, be, nb: (layer, be[b], 0, 0))
    grid_spec = pltpu.PrefetchScalarGridSpec(
        num_scalar_prefetch=2,
        grid=(n_rows // tm,),
        in_specs=[pl.BlockSpec((tm, d), lambda b, be, nb: (jnp.minimum(b, nb[0] - 1), 0)),
                  expert(d, de), expert(1, de), expert(d, de), expert(1, de),
                  expert(de, d), expert(1, d)],
        out_specs=pl.BlockSpec((tm, d), lambda b, be, nb: (b, 0)),
        scratch_shapes=[pltpu.VMEM((d, de), BF16), pltpu.VMEM((d, de), BF16),
                        pltpu.VMEM((de, d), BF16)],
    )
    return pl.pallas_call(
        _moe_kernel,
        grid_spec=grid_spec,
        out_shape=jax.ShapeDtypeStruct((n_rows, d), F32),
        compiler_params=_params("arbitrary"),
        name="moe_experts",
    )(block_expert, blocks_used, xs, w_gate, b_gate.reshape(DEPTH, N_EXPERTS, 1, de),
      w_up, b_up.reshape(DEPTH, N_EXPERTS, 1, de), w_down, b_down.reshape(DEPTH, N_EXPERTS, 1, d))


def _combine_kernel(dest_ref, y_hbm, gate_ref, h_ref, g_ref, b_ref, o32_ref, o16_ref,
                    buf_ref, sem_ref):
    tb = COMBINE_ROWS
    i = pl.program_id(0)
    steps = pl.num_programs(0)

    def start_gather(block, slot):
        def issue(t, carry):
            for k in range(TOP_K):
                row = dest_ref[(block * tb + t) * TOP_K + k]
                pltpu.make_async_copy(y_hbm.at[pl.ds(row, 1), :],
                                      buf_ref.at[slot, k, pl.ds(t, 1), :],
                                      sem_ref.at[slot]).start()
            return carry
        lax.fori_loop(0, tb, issue, 0, unroll=2)

    def wait_gather(slot):
        for k in range(TOP_K):
            pltpu.make_async_copy(y_hbm.at[pl.ds(0, tb), :], buf_ref.at[slot, k],
                                  sem_ref.at[slot]).wait()

    slot = i % 2

    @pl.when(i == 0)
    def _():
        start_gather(0, 0)

    @pl.when(i + 1 < steps)
    def _():
        start_gather(i + 1, 1 - slot)

    wait_gather(slot)
    gates = gate_ref[...]
    ffn = buf_ref[slot, 0] * gates[:, 0:1]
    for k in range(1, TOP_K):
        ffn = ffn + buf_ref[slot, k] * gates[:, k:k + 1]
    out = _deepnorm_layer_norm(h_ref[...], ffn, g_ref[...], b_ref[...])
    o32_ref[...] = out
    o16_ref[...] = out.astype(BF16)


def _combine_ln(y, dest, gates, h32, g, b):
    n, d = h32.shape
    tb = COMBINE_ROWS
    grid_spec = pltpu.PrefetchScalarGridSpec(
        num_scalar_prefetch=1,
        grid=(n // tb,),
        in_specs=[pl.BlockSpec(memory_space=pl.ANY),
                  pl.BlockSpec((tb, LANES), lambda i, dest: (i, 0)),
                  pl.BlockSpec((tb, d), lambda i, dest: (i, 0)),
                  pl.BlockSpec((1, d), lambda i, dest: (0, 0)),
                  pl.BlockSpec((1, d), lambda i, dest: (0, 0))],
        out_specs=[pl.BlockSpec((tb, d), lambda i, dest: (i, 0)),
                   pl.BlockSpec((tb, d), lambda i, dest: (i, 0))],
        scratch_shapes=[pltpu.VMEM((2, TOP_K, tb, d), F32), pltpu.SemaphoreType.DMA((2,))],
    )
    return pl.pallas_call(
        _combine_kernel,
        grid_spec=grid_spec,
        out_shape=[jax.ShapeDtypeStruct((n, d), F32), jax.ShapeDtypeStruct((n, d), BF16)],
        compiler_params=_params("arbitrary"),
        name="combine_ln",
    )(dest, y, gates, h32, g.reshape(1, d), b.reshape(1, d))


def _routed_ffn_ln(h32, layer, xs_buffer, router_w, router_b, w_gate, b_gate, w_up, b_up, w_down,
                   b_down, g, b):
    idx_pad, gate_pad, rank_pad, counts = _router(h32, router_w, router_b)
    dest, block_expert, blocks_used = _routing_tables(idx_pad[:, :TOP_K], rank_pad[:, :TOP_K],
                                                      counts[0, :N_EXPERTS])
    xs = _dispatch(h32, dest, xs_buffer)
    y = _moe_experts(xs, block_expert, blocks_used, layer,
                     w_gate, b_gate, w_up, b_up, w_down, b_down)
    h32, h16 = _combine_ln(y, dest, gate_pad, h32, g, b)
    return h32, h16, xs


def kernel(x, positions, ln_g, ln_b, sb_w_in, sb_w_out, sc_w_in, sc_conv_w, sc_w_out, mla_w_in, mla_q_norm, mla_w_uq, mla_kv_norm, mla_w_ukv, mla_w_out, router_w, router_b, moe_w_gate, moe_b_gate, moe_w_up, moe_b_up, moe_w_down, moe_b_down):
    batch, seq, d = x.shape
    n = batch * seq
    h32 = x.reshape(n, d)
    h16 = h32.astype(BF16)
    xs_buffer = jnp.zeros((n * TOP_K + N_EXPERTS * MOE_ROWS, d), F32)
    for layer in range(DEPTH):
        kind, slot = layer % N_MIXERS, layer // N_MIXERS
        g0, b0 = ln_g[layer, 0], ln_b[layer, 0]
        if kind == 0:
            qkv = _proj(h16, sb_w_in[slot], BF16, tn=d)
            attn = _sb_attention(qkv, batch, seq)
            h32, h16 = _outproj_ln(attn, sb_w_out[slot], h32, g0, b0)
        elif kind == 1:
            proj = _proj(h16, sc_w_in[slot], F32, tn=d)
            h32, h16 = _conv_outproj_ln(proj, sc_conv_w[slot], sc_w_out[slot], h32, g0, b0, seq)
        else:
            q, k, v = _mla_prep(h16, positions, mla_w_in[slot], mla_q_norm[slot], mla_w_uq[slot],
                                mla_kv_norm[slot], mla_w_ukv[slot], batch, seq)
            attn = _mla_attention(q, k, v)
            h32, h16 = _outproj_ln(attn, mla_w_out[slot], h32, g0, b0)
        h32, h16, xs_buffer = _routed_ffn_ln(h32, layer, xs_buffer, router_w[layer], router_b[layer],
                                             moe_w_gate, moe_b_gate, moe_w_up, moe_b_up, moe_w_down,
                                             moe_b_down, ln_g[layer, 1], ln_b[layer, 1])
    return h32.reshape(batch, seq, d)
```

```python
import functools
import math

import jax
import jax.numpy as jnp
from jax import lax
from jax.experimental import pallas as pl
from jax.experimental.pallas import tpu as pltpu

F32 = jnp.float32
BF16 = jnp.bfloat16

D_MODEL = 1024
DEPTH = 4
N_MIXERS = 3
SB_HEADS = 16
SB_HEAD_DIM = D_MODEL // SB_HEADS
SB_SCALE = 1.0 / math.sqrt(SB_HEAD_DIM)
SC_WIDTH = 3
MLA_HEADS = 8
QK_NOPE = 128
QK_ROPE = 64
V_HEAD = 128
Q_LORA = 384
KV_LORA = 256
MLA_SCALE = 1.0 / math.sqrt(QK_NOPE + QK_ROPE)
ROPE_THETA = 10000.0
N_EXPERTS = 32
TOP_K = 4
SWIGLU_LIMIT = 7.0
SWIGLU_ALPHA = 1.702
LN_EPS = 1e-5
RMS_EPS = 1e-6
DEEPNORM_ALPHA = (2 * DEPTH) ** 0.25
LOG2E = math.log2(math.e)
SB_UNDERFLOW = 110.0

LANES = 128
SUBLANES = 8
VMEM_LIMIT_BYTES = 56 * 1024 * 1024

ROW_TILE = 512
SB_TILE = 256
SB_HEADS_PER_STEP = 4
MLA_TILE = 512
MLA_HEADS_PER_STEP = 2
MOE_ROWS = 512
COMBINE_ROWS = 128
DISPATCH_ROWS = 256
MLA_QK_PAD = 256


def _params(*semantics):
    return pltpu.CompilerParams(dimension_semantics=semantics,
                                vmem_limit_bytes=VMEM_LIMIT_BYTES)


def _deepnorm_layer_norm(h, sub, g, b):
    y = DEEPNORM_ALPHA * h + sub
    mu = jnp.mean(y, axis=-1, keepdims=True)
    d = y - mu
    var = jnp.mean(d * d, axis=-1, keepdims=True)
    return d * lax.rsqrt(var + LN_EPS) * g + b


def _proj_kernel(x_ref, w_ref, o_ref, w16_ref):
    @pl.when(pl.program_id(1) == 0)
    def _():
        w16_ref[...] = w_ref[...].astype(BF16)

    o_ref[...] = jnp.dot(x_ref[...], w16_ref[...],
                         preferred_element_type=F32).astype(o_ref.dtype)


def _proj(x16, w32, out_dtype, tn):
    n, k = x16.shape
    nout = w32.shape[1]
    tm = min(ROW_TILE, n)
    return pl.pallas_call(
        _proj_kernel,
        grid=(nout // tn, n // tm),
        in_specs=[pl.BlockSpec((tm, k), lambda j, i: (i, 0)),
                  pl.BlockSpec((k, tn), lambda j, i: (0, j))],
        out_specs=pl.BlockSpec((tm, tn), lambda j, i: (i, j)),
        out_shape=jax.ShapeDtypeStruct((n, nout), out_dtype),
        scratch_shapes=[pltpu.VMEM((k, tn), BF16)],
        compiler_params=_params("arbitrary", "arbitrary"),
        name="proj",
    )(x16, w32)


def _outproj_ln_kernel(a_ref, w_ref, h_ref, g_ref, b_ref, o32_ref, o16_ref, w16_ref):
    @pl.when(pl.program_id(0) == 0)
    def _():
        w16_ref[...] = w_ref[...].astype(BF16)

    mix = jnp.dot(a_ref[...], w16_ref[...], preferred_element_type=F32)
    out = _deepnorm_layer_norm(h_ref[...], mix, g_ref[...], b_ref[...])
    o32_ref[...] = out
    o16_ref[...] = out.astype(BF16)


def _outproj_ln(a16, w32, h32, g, b):
    n, k = a16.shape
    d = w32.shape[1]
    tm = min(ROW_TILE, n)
    return pl.pallas_call(
        _outproj_ln_kernel,
        grid=(n // tm,),
        in_specs=[pl.BlockSpec((tm, k), lambda i: (i, 0)),
                  pl.BlockSpec((k, d), lambda i: (0, 0)),
                  pl.BlockSpec((tm, d), lambda i: (i, 0)),
                  pl.BlockSpec((1, d), lambda i: (0, 0)),
                  pl.BlockSpec((1, d), lambda i: (0, 0))],
        out_specs=[pl.BlockSpec((tm, d), lambda i: (i, 0)),
                   pl.BlockSpec((tm, d), lambda i: (i, 0))],
        out_shape=[jax.ShapeDtypeStruct((n, d), F32),
                   jax.ShapeDtypeStruct((n, d), BF16)],
        scratch_shapes=[pltpu.VMEM((k, d), BF16)],
        compiler_params=_params("arbitrary"),
        name="outproj_ln",
    )(a16, w32, h32, g.reshape(1, d), b.reshape(1, d))


def _sb_kernel(q_ref, k_ref, v_ref, o_ref, acc_ref, surv_ref):
    t = SB_TILE
    i = pl.program_id(2)
    lane = lax.broadcasted_iota(jnp.int32, (t, LANES), 1)
    first_head = lane < SB_HEAD_DIM
    heads = range(SB_HEADS_PER_STEP)
    pair_lanes = lambda head: slice((head // 2) * LANES, (head // 2 + 1) * LANES)
    q_heads = []
    for head in heads:
        q = q_ref[:, pair_lanes(head)] * SB_SCALE
        keep = first_head if head % 2 == 0 else jnp.logical_not(first_head)
        q_heads.append(jnp.where(keep, q, jnp.zeros_like(q)))
    row = lax.broadcasted_iota(jnp.int32, (t, t), 0)
    col = lax.broadcasted_iota(jnp.int32, (t, t), 1)
    strictly_before = col < row
    later_key = jnp.where(row > col, 1.0, 0.0).astype(BF16)

    def scores(head, kblk):
        z = lax.dot_general(q_heads[head], kblk, (((1,), (1,)), ((), ())),
                            preferred_element_type=F32)
        sp = jnp.maximum(z, 0.0) + jnp.log(1.0 + jnp.exp(-jnp.abs(z)))
        return z, sp

    def suffix_sums(parts):
        stacked = jnp.concatenate([p.astype(BF16) for p in parts], axis=0)
        sums = jnp.dot(stacked, later_key, preferred_element_type=F32)
        return [sums[c * t:(c + 1) * t] for c in range(len(parts))]

    def diagonal(head, kblk, vblk):
        z, sp = scores(head, kblk)
        spm = jnp.where(strictly_before, sp, 0.0)
        (cs,) = suffix_sums([spm])
        a = jnp.where(strictly_before, jnp.exp(z - sp - cs), 0.0)
        acc_ref[head] = jnp.dot(a.astype(BF16), vblk, preferred_element_type=F32)
        surv_ref[head] = jnp.sum(spm, axis=1, keepdims=True)

    def diagonal_and_previous(head, kblk, vblk):
        z, sp = scores(head, kblk)
        spm = jnp.where(strictly_before, sp[:, t:], 0.0)
        cs_p, cs_d = suffix_sums([sp[:, :t], spm])
        surv_p = jnp.sum(spm, axis=1, keepdims=True)
        e = z - sp
        a = jnp.concatenate([jnp.exp(e[:, :t] - cs_p - surv_p),
                             jnp.where(strictly_before, jnp.exp(e[:, t:] - cs_d), 0.0)], axis=1)
        acc_ref[head] = jnp.dot(a.astype(BF16), vblk, preferred_element_type=F32)
        surv_ref[head] = surv_p + jnp.sum(sp[:, :t], axis=1, keepdims=True)

    def single(head, kblk, vblk):
        z, sp = scores(head, kblk)
        (cs,) = suffix_sums([sp])
        surv = surv_ref[head]
        a = jnp.exp(z - sp - cs - surv)
        acc_ref[head] += jnp.dot(a.astype(BF16), vblk, preferred_element_type=F32)
        surv_ref[head] = surv + jnp.sum(sp, axis=1, keepdims=True)

    def keep_going():
        least = jnp.min(surv_ref[0])
        for head in heads[1:]:
            least = jnp.minimum(least, jnp.min(surv_ref[head]))
        return (least < SB_UNDERFLOW).astype(jnp.int32)

    def sweep(tile_fn, start, rows):
        for head in heads:
            tile_fn(head, k_ref[pl.ds(start, rows), pair_lanes(head)],
                    v_ref[pl.ds(start, rows), pair_lanes(head)])

    @pl.when(i == 0)
    def _():
        sweep(diagonal, 0, t)

    @pl.when(i > 0)
    def _():
        sweep(diagonal_and_previous, pl.multiple_of((i - 1) * t, t), 2 * t)

    def cond(carry):
        step, go = carry
        return jnp.logical_and(step < i - 1, go > 0)

    def body(carry):
        step, _ = carry
        sweep(single, pl.multiple_of((i - 2 - step) * t, t), t)
        return step + 1, keep_going()

    lax.while_loop(cond, body, (jnp.int32(0), keep_going()))
    for head in heads[::2]:
        o_ref[:, pair_lanes(head)] = jnp.where(first_head, acc_ref[head],
                                               acc_ref[head + 1]).astype(o_ref.dtype)


def _sb_attention(qkv16, batch, seq):
    t = SB_TILE
    d = D_MODEL
    hs = SB_HEADS_PER_STEP
    width = hs * SB_HEAD_DIM
    groups = d // width
    qkv = qkv16.reshape(batch, seq, 3 * d)
    out = pl.pallas_call(
        _sb_kernel,
        grid=(batch, groups, seq // t),
        in_specs=[pl.BlockSpec((None, t, width), lambda b, p, i: (b, i, p)),
                  pl.BlockSpec((None, seq, width), lambda b, p, i: (b, 0, groups + p)),
                  pl.BlockSpec((None, seq, width), lambda b, p, i: (b, 0, 2 * groups + p))],
        out_specs=pl.BlockSpec((None, t, width), lambda b, p, i: (b, i, p)),
        out_shape=jax.ShapeDtypeStruct((batch, seq, d), BF16),
        scratch_shapes=[pltpu.VMEM((hs, t, LANES), F32), pltpu.VMEM((hs, t, 1), F32)],
        compiler_params=_params("arbitrary", "arbitrary", "arbitrary"),
        name="sb_attention",
    )(qkv, qkv, qkv)
    return out.reshape(batch * seq, d)


def _conv_kernel(gb_ref, gc_ref, hh_ref, pc_ref, ph_ref, cw_ref, w_ref, h_ref, g_ref, b_ref,
                 o32_ref, o16_ref, w16_ref, *, blocks_per_seq):
    i = pl.program_id(0)

    @pl.when(i == 0)
    def _():
        w16_ref[...] = w_ref[...].astype(BF16)

    u = gc_ref[...] * hh_ref[...]
    halo = pc_ref[...] * ph_ref[...]
    halo = jnp.where(i % blocks_per_seq == 0, jnp.zeros_like(halo), halo)
    prev1 = halo[SUBLANES - 1:SUBLANES, :]
    prev2 = halo[SUBLANES - 2:SUBLANES - 1, :]
    row = lax.broadcasted_iota(jnp.int32, u.shape, 0)
    u1 = jnp.where(row == 0, prev1, pltpu.roll(u, 1, 0))
    u2 = jnp.where(row == 0, prev2, jnp.where(row == 1, prev1, pltpu.roll(u, 2, 0)))
    conv = cw_ref[0:1, :] * u2 + cw_ref[1:2, :] * u1 + cw_ref[2:3, :] * u
    gated = (gb_ref[...] * conv).astype(BF16)
    mix = jnp.dot(gated, w16_ref[...], preferred_element_type=F32)
    out = _deepnorm_layer_norm(h_ref[...], mix, g_ref[...], b_ref[...])
    o32_ref[...] = out
    o16_ref[...] = out.astype(BF16)


def _conv_outproj_ln(proj32, conv_w, w16, h32, g, b, seq):
    n = proj32.shape[0]
    d = D_MODEL
    tm = min(ROW_TILE, seq)
    halo_blocks = tm // SUBLANES
    kern = functools.partial(_conv_kernel, blocks_per_seq=seq // tm)
    return pl.pallas_call(
        kern,
        grid=(n // tm,),
        in_specs=[pl.BlockSpec((tm, d), lambda i: (i, 0)),
                  pl.BlockSpec((tm, d), lambda i: (i, 1)),
                  pl.BlockSpec((tm, d), lambda i: (i, 2)),
                  pl.BlockSpec((SUBLANES, d), lambda i: (jnp.maximum(i * halo_blocks - 1, 0), 1)),
                  pl.BlockSpec((SUBLANES, d), lambda i: (jnp.maximum(i * halo_blocks - 1, 0), 2)),
                  pl.BlockSpec((SC_WIDTH, d), lambda i: (0, 0)),
                  pl.BlockSpec((d, d), lambda i: (0, 0)),
                  pl.BlockSpec((tm, d), lambda i: (i, 0)),
                  pl.BlockSpec((1, d), lambda i: (0, 0)),
                  pl.BlockSpec((1, d), lambda i: (0, 0))],
        out_specs=[pl.BlockSpec((tm, d), lambda i: (i, 0)),
                   pl.BlockSpec((tm, d), lambda i: (i, 0))],
        out_shape=[jax.ShapeDtypeStruct((n, d), F32),
                   jax.ShapeDtypeStruct((n, d), BF16)],
        scratch_shapes=[pltpu.VMEM((d, d), BF16)],
        compiler_params=_params("arbitrary"),
        name="conv_outproj_ln",
    )(proj32, proj32, proj32, proj32, proj32, conv_w, w16, h32, g.reshape(1, d), b.reshape(1, d))


def _rms_norm(x, g):
    ms = jnp.mean(x * x, axis=-1, keepdims=True)
    return x * lax.rsqrt(ms + RMS_EPS) * g


def _mla_prep_kernel(x_ref, pos_ref, freq_ref, sign_ref, win_ref, qn_ref, kvn_ref, wuq_ref, wukv_ref,
                     q_ref, k_ref, v_ref):
    lat = jnp.dot(x_ref[...], win_ref[...], preferred_element_type=F32)
    q_lat = lat[:, :Q_LORA]
    kv_lat = lat[:, Q_LORA:Q_LORA + KV_LORA]
    r0 = Q_LORA + KV_LORA
    k_r = lat[:, r0:r0 + QK_ROPE]
    k_r_swapped = lat[:, r0 + QK_ROPE:r0 + 2 * QK_ROPE]
    qn = _rms_norm(q_lat, qn_ref[...]).astype(BF16)
    kvn = _rms_norm(kv_lat, kvn_ref[...]).astype(BF16)
    qq = jnp.dot(qn, wuq_ref[...], preferred_element_type=F32)
    kv = jnp.dot(kvn, wukv_ref[...], preferred_element_type=F32)
    ang = pos_ref[...].astype(F32) * freq_ref[...]
    cc = jnp.cos(ang)
    ss = jnp.sin(ang) * sign_ref[...]
    k_rope = (k_r * cc + k_r_swapped * ss).astype(BF16)
    pad = jnp.zeros((x_ref.shape[0], MLA_QK_PAD - QK_NOPE - QK_ROPE), BF16)
    for h in range(MLA_HEADS):
        c = h * MLA_QK_PAD
        q_nope = qq[:, c:c + QK_NOPE]
        q_r = qq[:, c + QK_NOPE:c + QK_NOPE + QK_ROPE]
        q_r_swapped = qq[:, c + QK_NOPE + QK_ROPE:c + MLA_QK_PAD]
        q_rope = q_r * cc + q_r_swapped * ss
        q_ref[h, :, 0:QK_NOPE] = (q_nope * (MLA_SCALE * LOG2E)).astype(BF16)
        q_ref[h, :, QK_NOPE:QK_NOPE + QK_ROPE] = (q_rope * (MLA_SCALE * LOG2E)).astype(BF16)
        q_ref[h, :, QK_NOPE + QK_ROPE:MLA_QK_PAD] = pad
        c2 = h * (QK_NOPE + V_HEAD)
        k_ref[h, :, 0:QK_NOPE] = kv[:, c2:c2 + QK_NOPE].astype(BF16)
        k_ref[h, :, QK_NOPE:QK_NOPE + QK_ROPE] = k_rope
        k_ref[h, :, QK_NOPE + QK_ROPE:MLA_QK_PAD] = pad
        v_ref[h] = kv[:, c2 + QK_NOPE:c2 + QK_NOPE + V_HEAD].astype(BF16)


def _swap_halves(w):
    half = w.shape[-1] // 2
    return jnp.concatenate([w[..., half:], w[..., :half]], axis=-1)


def _mla_prep(h16, positions, w_in, q_norm, w_uq, kv_norm, w_ukv, batch, seq):
    n = h16.shape[0]
    d = D_MODEL
    tm = min(ROW_TILE, seq)
    r0 = Q_LORA + KV_LORA
    win = jnp.concatenate([w_in, _swap_halves(w_in[:, r0:r0 + QK_ROPE])], axis=1).astype(BF16)
    wq = w_uq.reshape(Q_LORA, MLA_HEADS, QK_NOPE + QK_ROPE)
    wq = jnp.concatenate([wq, _swap_halves(wq[:, :, QK_NOPE:])], axis=2)
    wq = wq.reshape(Q_LORA, MLA_HEADS * MLA_QK_PAD).astype(BF16)
    wkv = w_ukv.astype(BF16)
    inv_freq = ROPE_THETA ** (-jnp.arange(0, QK_ROPE, 2, dtype=F32) / QK_ROPE)
    freq = jnp.concatenate([inv_freq, inv_freq]).reshape(1, QK_ROPE)
    sign = jnp.concatenate([-jnp.ones((QK_ROPE // 2,), F32),
                            jnp.ones((QK_ROPE // 2,), F32)]).reshape(1, QK_ROPE)
    bps = seq // tm
    head_spec = lambda width: pl.BlockSpec((None, MLA_HEADS, tm, width),
                                           lambda i: (i // bps, 0, i % bps, 0))
    const = lambda shape: pl.BlockSpec(shape, lambda i: (0,) * len(shape))
    return pl.pallas_call(
        _mla_prep_kernel,
        grid=(n // tm,),
        in_specs=[pl.BlockSpec((tm, d), lambda i: (i, 0)),
                  pl.BlockSpec((tm, 1), lambda i: (i, 0)),
                  const((1, QK_ROPE)), const((1, QK_ROPE)),
                  const(win.shape), const((1, Q_LORA)), const((1, KV_LORA)),
                  const(wq.shape), const(wkv.shape)],
        out_specs=[head_spec(MLA_QK_PAD), head_spec(MLA_QK_PAD), head_spec(V_HEAD)],
        out_shape=[jax.ShapeDtypeStruct((batch, MLA_HEADS, seq, MLA_QK_PAD), BF16),
                   jax.ShapeDtypeStruct((batch, MLA_HEADS, seq, MLA_QK_PAD), BF16),
                   jax.ShapeDtypeStruct((batch, MLA_HEADS, seq, V_HEAD), BF16)],
        compiler_params=_params("arbitrary"),
        name="mla_prep",
    )(h16, positions.reshape(n, 1), freq, sign, win, q_norm.reshape(1, Q_LORA),
      kv_norm.reshape(1, KV_LORA), wq, wkv)


def _mla_attn_kernel(q_ref, k_ref, v_ref, o_ref, m_ref, l_ref, acc_ref, s_ref, p_ref, scale_ref):
    t = MLA_TILE
    i = pl.program_id(2)
    heads = range(MLA_HEADS_PER_STEP)
    m_ref[...] = jnp.full(m_ref.shape, -jnp.inf, F32)
    l_ref[...] = jnp.zeros(l_ref.shape, F32)
    acc_ref[...] = jnp.zeros(acc_ref.shape, F32)
    p_ref[1] = jnp.zeros(p_ref.shape[1:], BF16)
    scale_ref[1] = jnp.ones(scale_ref.shape[1:], F32)

    def scores(j, slot):
        start = pl.multiple_of(j * t, t)
        for head in heads:
            s_ref[slot, head] = lax.dot_general(
                q_ref[head], k_ref[head, pl.ds(start, t), :], (((1,), (1,)), ((), ())),
                preferred_element_type=F32)

    def accumulate(j, slot):
        start = pl.multiple_of(j * t, t)
        for head in heads:
            pv = jnp.dot(p_ref[slot, head], v_ref[head, pl.ds(start, t), :],
                         preferred_element_type=F32)
            acc_ref[head] = scale_ref[slot, head] * acc_ref[head] + pv

    def softmax(slot, diagonal):
        for head in heads:
            s = s_ref[slot, head]
            if diagonal:
                row = lax.broadcasted_iota(jnp.int32, (t, t), 0)
                col = lax.broadcasted_iota(jnp.int32, (t, t), 1)
                s = jnp.where(col <= row, s, -jnp.inf)
            m_prev = m_ref[head]
            m_new = jnp.maximum(m_prev, jnp.max(s, axis=1, keepdims=True))
            scale = jnp.exp2(m_prev - m_new)
            p = jnp.exp2(s - m_new)
            l_ref[head] = scale * l_ref[head] + jnp.sum(p, axis=1, keepdims=True)
            m_ref[head] = m_new
            scale_ref[slot, head] = scale
            p_ref[slot, head] = p.astype(BF16)

    scores(0, 0)

    def body(jj, carry):
        j = 2 * jj
        scores(j + 1, 1)
        accumulate(jnp.maximum(j - 1, 0), 1)
        softmax(0, False)
        scores(j + 2, 0)
        accumulate(j, 0)
        softmax(1, False)
        return carry

    lax.fori_loop(0, i // 2, body, 0)

    @pl.when(i % 2 == 0)
    def _():
        accumulate(jnp.maximum(i - 1, 0), 1)
        softmax(0, True)
        accumulate(i, 0)

    @pl.when(i % 2 == 1)
    def _():
        scores(i, 1)
        accumulate(jnp.maximum(i - 2, 0), 1)
        softmax(0, False)
        accumulate(i - 1, 0)
        softmax(1, True)
        accumulate(i, 1)

    for head in heads:
        o_ref[:, head * V_HEAD:(head + 1) * V_HEAD] = (
            acc_ref[head] / l_ref[head]).astype(o_ref.dtype)


def _mla_attention(q, k, v):
    batch, heads, seq, _ = q.shape
    t = MLA_TILE
    hp = MLA_HEADS_PER_STEP
    out = pl.pallas_call(
        _mla_attn_kernel,
        grid=(batch, heads // hp, seq // t),
        in_specs=[pl.BlockSpec((None, hp, t, MLA_QK_PAD), lambda b, h, i: (b, h, i, 0)),
                  pl.BlockSpec((None, hp, seq, MLA_QK_PAD), lambda b, h, i: (b, h, 0, 0)),
                  pl.BlockSpec((None, hp, seq, V_HEAD), lambda b, h, i: (b, h, 0, 0))],
        out_specs=pl.BlockSpec((None, t, hp * V_HEAD), lambda b, h, i: (b, i, h)),
        out_shape=jax.ShapeDtypeStruct((batch, seq, heads * V_HEAD), BF16),
        scratch_shapes=[pltpu.VMEM((hp, t, 1), F32), pltpu.VMEM((hp, t, 1), F32),
                        pltpu.VMEM((hp, t, V_HEAD), F32),
                        pltpu.VMEM((2, hp, t, t), F32), pltpu.VMEM((2, hp, t, t), BF16),
                        pltpu.VMEM((2, hp, t, 1), F32)],
        compiler_params=_params("arbitrary", "arbitrary", "arbitrary"),
        name="mla_attention",
    )(q, k, v)
    return out.reshape(batch * seq, heads * V_HEAD)


def _router_kernel(h_ref, whi_ref, wlo_ref, b_ref, idx_ref, gate_ref, rank_ref, count_ref, seen_ref):
    @pl.when(pl.program_id(0) == 0)
    def _():
        seen_ref[...] = jnp.zeros(seen_ref.shape, F32)

    x = h_ref[...]
    xhi = x.astype(BF16)
    xlo = (x - xhi.astype(F32)).astype(BF16)
    logits = (jnp.dot(xhi, whi_ref[...], preferred_element_type=F32)
              + jnp.dot(xhi, wlo_ref[...], preferred_element_type=F32)
              + jnp.dot(xlo, whi_ref[...], preferred_element_type=F32)) + b_ref[...]
    lane = lax.broadcasted_iota(jnp.int32, logits.shape, 1)
    lane_f = lane.astype(F32)
    vals = jnp.where(lane < N_EXPERTS, logits, -jnp.inf)
    idx_out = jnp.zeros(logits.shape, jnp.int32)
    gate_out = jnp.zeros(logits.shape, F32)
    picks = []
    top = None
    denom = None
    for k in range(TOP_K):
        m = jnp.max(vals, axis=1, keepdims=True)
        pick = jnp.min(jnp.where(vals == m, lane_f, float(LANES)), axis=1, keepdims=True)
        pick_i = pick.astype(jnp.int32)
        picks.append(pick_i)
        if k == 0:
            top = m
            e = jnp.ones_like(m)
            denom = e
        else:
            e = jnp.exp(m - top)
            denom = denom + e
        idx_out = jnp.where(lane == k, pick_i, idx_out)
        gate_out = jnp.where(lane == k, e, gate_out)
        vals = jnp.where(lane == pick_i, -jnp.inf, vals)
    idx_ref[...] = idx_out
    gate_ref[...] = gate_out / denom

    chosen = jnp.zeros(logits.shape, F32)
    for pick_i in picks:
        chosen = jnp.where(lane == pick_i, 1.0, chosen)
    tm = logits.shape[0]
    earlier = (lax.broadcasted_iota(jnp.int32, (tm, tm), 1)
               < lax.broadcasted_iota(jnp.int32, (tm, tm), 0))
    before = jnp.dot(jnp.where(earlier, 1.0, 0.0).astype(BF16), chosen.astype(BF16),
                     preferred_element_type=F32) + seen_ref[...]
    rank_out = jnp.zeros(logits.shape, jnp.int32)
    for k, pick_i in enumerate(picks):
        rank_k = jnp.sum(jnp.where(lane == pick_i, before, 0.0), axis=1, keepdims=True)
        rank_out = jnp.where(lane == k, rank_k.astype(jnp.int32), rank_out)
    rank_ref[...] = rank_out
    seen = seen_ref[...] + jnp.sum(chosen, axis=0, keepdims=True)
    seen_ref[...] = seen
    count_ref[...] = seen.astype(jnp.int32)


def _router(h32, router_w, router_b):
    n, d = h32.shape
    tm = min(ROW_TILE, n)
    w = jnp.zeros((d, LANES), F32).at[:, :N_EXPERTS].set(router_w)
    whi = w.astype(BF16)
    wlo = (w - whi.astype(F32)).astype(BF16)
    b = jnp.zeros((1, LANES), F32).at[0, :N_EXPERTS].set(router_b)
    tok = lambda: pl.BlockSpec((tm, LANES), lambda i: (i, 0))
    return pl.pallas_call(
        _router_kernel,
        grid=(n // tm,),
        in_specs=[pl.BlockSpec((tm, d), lambda i: (i, 0)),
                  pl.BlockSpec((d, LANES), lambda i: (0, 0)),
                  pl.BlockSpec((d, LANES), lambda i: (0, 0)),
                  pl.BlockSpec((1, LANES), lambda i: (0, 0))],
        out_specs=[tok(), tok(), tok(), pl.BlockSpec((1, LANES), lambda i: (0, 0))],
        out_shape=[jax.ShapeDtypeStruct((n, LANES), jnp.int32),
                   jax.ShapeDtypeStruct((n, LANES), F32),
                   jax.ShapeDtypeStruct((n, LANES), jnp.int32),
                   jax.ShapeDtypeStruct((1, LANES), jnp.int32)],
        scratch_shapes=[pltpu.VMEM((1, LANES), F32)],
        compiler_params=_params("arbitrary"),
        name="router",
    )(h32, whi, wlo, b)


def _routing_tables(top_idx, rank, counts):
    blk = MOE_ROWS
    n = top_idx.shape[0]
    n_blocks = (n * TOP_K + N_EXPERTS * blk) // blk
    padded = (counts + blk - 1) // blk * blk
    padded_end = jnp.cumsum(padded)
    padded_start = padded_end - padded
    experts = jnp.arange(N_EXPERTS, dtype=jnp.int32)
    start_of = jnp.sum(jnp.where(top_idx[..., None] == experts, padded_start, 0), axis=-1)
    dest = (start_of + rank).reshape(-1).astype(jnp.int32)
    block_row = jnp.arange(n_blocks, dtype=jnp.int32)[:, None] * blk
    block_expert = jnp.minimum(jnp.sum((padded_end[None, :] <= block_row).astype(jnp.int32), axis=1),
                               N_EXPERTS - 1).astype(jnp.int32)
    blocks_used = (padded_end[-1] // blk).astype(jnp.int32).reshape(1)
    return dest, block_expert, blocks_used


def _dispatch_kernel(dest_ref, h_ref, xs_in_hbm, xs_hbm, sem_ref):
    del xs_in_hbm
    tb = DISPATCH_ROWS
    i = pl.program_id(0)

    def issue(t, carry):
        for k in range(TOP_K):
            row = dest_ref[(i * tb + t) * TOP_K + k]
            pltpu.make_async_copy(h_ref.at[pl.ds(t, 1), :], xs_hbm.at[pl.ds(row, 1), :],
                                  sem_ref.at[0]).start()
        return carry

    lax.fori_loop(0, tb, issue, 0, unroll=8)
    for k in range(TOP_K):
        pltpu.make_async_copy(h_ref, xs_hbm.at[pl.ds(0, tb), :], sem_ref.at[0]).wait()


def _dispatch(h32, dest, xs_buffer):
    n, d = h32.shape
    n_rows = xs_buffer.shape[0]
    tb = DISPATCH_ROWS
    grid_spec = pltpu.PrefetchScalarGridSpec(
        num_scalar_prefetch=1,
        grid=(n // tb,),
        in_specs=[pl.BlockSpec((tb, d), lambda i, dest: (i, 0)),
                  pl.BlockSpec(memory_space=pl.ANY)],
        out_specs=pl.BlockSpec(memory_space=pl.ANY),
        scratch_shapes=[pltpu.SemaphoreType.DMA((1,))],
    )
    return pl.pallas_call(
        _dispatch_kernel,
        grid_spec=grid_spec,
        out_shape=jax.ShapeDtypeStruct((n_rows, d), F32),
        input_output_aliases={2: 0},
        compiler_params=_params("arbitrary"),
        name="dispatch",
    )(dest, h32, xs_buffer)


def _moe_kernel(be_ref, nb_ref, x_ref, wg_ref, bg_ref, wu_ref, bu_ref, wd_ref, bd_ref, y_ref,
                wg16_ref, wu16_ref, wd16_ref):
    b = pl.program_id(0)
    used = nb_ref[0]
    new_expert = jnp.logical_or(b == 0, be_ref[b] != be_ref[jnp.maximum(b - 1, 0)])

    @pl.when(jnp.logical_and(b < used, new_expert))
    def _():
        wg16_ref[...] = wg_ref[...].astype(BF16)
        wu16_ref[...] = wu_ref[...].astype(BF16)
        wd16_ref[...] = wd_ref[...].astype(BF16)

    @pl.when(b < used)
    def _():
        xb = x_ref[...].astype(BF16)
        g = jnp.minimum(jnp.dot(xb, wg16_ref[...], preferred_element_type=F32) + bg_ref[...],
                        SWIGLU_LIMIT)
        u = jnp.clip(jnp.dot(xb, wu16_ref[...], preferred_element_type=F32) + bu_ref[...],
                     -SWIGLU_LIMIT, SWIGLU_LIMIT)
        hdn = g * jax.nn.sigmoid(SWIGLU_ALPHA * g) * (u + 1.0)
        y_ref[...] = jnp.dot(hdn.astype(BF16), wd16_ref[...], preferred_element_type=F32) + bd_ref[...]

    @pl.when(b >= used)
    def _():
        y_ref[...] = jnp.zeros(y_ref.shape, F32)


def _moe_experts(xs, block_expert, blocks_used, layer, w_gate, b_gate, w_up, b_up, w_down, b_down):
    n_rows, d = xs.shape
    tm = MOE_ROWS
    de = w_gate.shape[3]
    expert = lambda rows, cols: pl.BlockSpec((None, None, rows, cols),
                                             lambda b, be, nb: (layer, be[b], 0, 0))
    grid_spec = pltpu.PrefetchScalarGridSpec(
        num_scalar_prefetch=2,
        grid=(n_rows // tm,),
        in_specs=[pl.BlockSpec((tm, d), lambda b, be, nb: (jnp.minimum(b, nb[0] - 1), 0)),
                  expert(d, de), expert(1, de), expert(d, de), expert(1, de),
                  expert(de, d), expert(1, d)],
        out_specs=pl.BlockSpec((tm, d), lambda b, be, nb: (b, 0)),
        scratch_shapes=[pltpu.VMEM((d, de), BF16), pltpu.VMEM((d, de), BF16),
                        pltpu.VMEM((de, d), BF16)],
    )
    return pl.pallas_call(
        _moe_kernel,
        grid_spec=grid_spec,
        out_shape=jax.ShapeDtypeStruct((n_rows, d), F32),
        compiler_params=_params("arbitrary"),
        name="moe_experts",
    )(block_expert, blocks_used, xs, w_gate, b_gate.reshape(DEPTH, N_EXPERTS, 1, de),
      w_up, b_up.reshape(DEPTH, N_EXPERTS, 1, de), w_down, b_down.reshape(DEPTH, N_EXPERTS, 1, d))


def _combine_kernel(dest_ref, y_hbm, gate_ref, h_ref, g_ref, b_ref, o32_ref, o16_ref,
                    buf_ref, sem_ref):
    tb = COMBINE_ROWS
    i = pl.program_id(0)
    steps = pl.num_programs(0)

    def start_gather(block, slot):
        def issue(t, carry):
            for k in range(TOP_K):
                row = dest_ref[(block * tb + t) * TOP_K + k]
                pltpu.make_async_copy(y_hbm.at[pl.ds(row, 1), :],
                                      buf_ref.at[slot, k, pl.ds(t, 1), :],
                                      sem_ref.at[slot]).start()
            return carry
        lax.fori_loop(0, tb, issue, 0, unroll=8)

    def wait_gather(slot):
        for k in range(TOP_K):
            pltpu.make_async_copy(y_hbm.at[pl.ds(0, tb), :], buf_ref.at[slot, k],
                                  sem_ref.at[slot]).wait()

    slot = i % 2

    @pl.when(i == 0)
    def _():
        start_gather(0, 0)

    @pl.when(i + 1 < steps)
    def _():
        start_gather(i + 1, 1 - slot)

    wait_gather(slot)
    gates = gate_ref[...]
    ffn = buf_ref[slot, 0] * gates[:, 0:1]
    for k in range(1, TOP_K):
        ffn = ffn + buf_ref[slot, k] * gates[:, k:k + 1]
    out = _deepnorm_layer_norm(h_ref[...], ffn, g_ref[...], b_ref[...])
    o32_ref[...] = out
    o16_ref[...] = out.astype(BF16)


def _combine_ln(y, dest, gates, h32, g, b):
    n, d = h32.shape
    tb = COMBINE_ROWS
    grid_spec = pltpu.PrefetchScalarGridSpec(
        num_scalar_prefetch=1,
        grid=(n // tb,),
        in_specs=[pl.BlockSpec(memory_space=pl.ANY),
                  pl.BlockSpec((tb, LANES), lambda i, dest: (i, 0)),
                  pl.BlockSpec((tb, d), lambda i, dest: (i, 0)),
                  pl.BlockSpec((1, d), lambda i, dest: (0, 0)),
                  pl.BlockSpec((1, d), lambda i, dest: (0, 0))],
        out_specs=[pl.BlockSpec((tb, d), lambda i, dest: (i, 0)),
                   pl.BlockSpec((tb, d), lambda i, dest: (i, 0))],
        scratch_shapes=[pltpu.VMEM((2, TOP_K, tb, d), F32), pltpu.SemaphoreType.DMA((2,))],
    )
    return pl.pallas_call(
        _combine_kernel,
        grid_spec=grid_spec,
        out_shape=[jax.ShapeDtypeStruct((n, d), F32), jax.ShapeDtypeStruct((n, d), BF16)],
        compiler_params=_params("arbitrary"),
        name="combine_ln",
    )(dest, y, gates, h32, g.reshape(1, d), b.reshape(1, d))


def _routed_ffn_ln(h32, layer, xs_buffer, router_w, router_b, w_gate, b_gate, w_up, b_up, w_down,
                   b_down, g, b):
    idx_pad, gate_pad, rank_pad, counts = _router(h32, router_w, router_b)
    dest, block_expert, blocks_used = _routing_tables(idx_pad[:, :TOP_K], rank_pad[:, :TOP_K],
                                                      counts[0, :N_EXPERTS])
    xs = _dispatch(h32, dest, xs_buffer)
    y = _moe_experts(xs, block_expert, blocks_used, layer,
                     w_gate, b_gate, w_up, b_up, w_down, b_down)
    h32, h16 = _combine_ln(y, dest, gate_pad, h32, g, b)
    return h32, h16, xs


def kernel(x, positions, ln_g, ln_b, sb_w_in, sb_w_out, sc_w_in, sc_conv_w, sc_w_out, mla_w_in, mla_q_norm, mla_w_uq, mla_kv_norm, mla_w_ukv, mla_w_out, router_w, router_b, moe_w_gate, moe_b_gate, moe_w_up, moe_b_up, moe_w_down, moe_b_down):
    batch, seq, d = x.shape
    n = batch * seq
    h32 = x.reshape(n, d)
    h16 = h32.astype(BF16)
    xs_buffer = jnp.zeros((n * TOP_K + N_EXPERTS * MOE_ROWS, d), F32)
    for layer in range(DEPTH):
        kind, slot = layer % N_MIXERS, layer // N_MIXERS
        g0, b0 = ln_g[layer, 0], ln_b[layer, 0]
        if kind == 0:
            qkv = _proj(h16, sb_w_in[slot], BF16, tn=d)
            attn = _sb_attention(qkv, batch, seq)
            h32, h16 = _outproj_ln(attn, sb_w_out[slot], h32, g0, b0)
        elif kind == 1:
            proj = _proj(h16, sc_w_in[slot], F32, tn=d)
            h32, h16 = _conv_outproj_ln(proj, sc_conv_w[slot], sc_w_out[slot], h32, g0, b0, seq)
        else:
            q, k, v = _mla_prep(h16, positions, mla_w_in[slot], mla_q_norm[slot], mla_w_uq[slot],
                                mla_kv_norm[slot], mla_w_ukv[slot], batch, seq)
            attn = _mla_attention(q, k, v)
            h32, h16 = _outproj_ln(attn, mla_w_out[slot], h32, g0, b0)
        h32, h16, xs_buffer = _routed_ffn_ln(h32, layer, xs_buffer, router_w[layer], router_b[layer],
                                             moe_w_gate, moe_b_gate, moe_w_up, moe_b_up, moe_w_down,
                                             moe_b_down, ln_g[layer, 1], ln_b[layer, 1])
    return h32.reshape(batch, seq, d)
```

```python
import functools
import math

import jax
import jax.numpy as jnp
from jax import lax
from jax.experimental import pallas as pl
from jax.experimental.pallas import tpu as pltpu

F32 = jnp.float32
BF16 = jnp.bfloat16

D_MODEL = 1024
DEPTH = 4
N_MIXERS = 3
SB_HEADS = 16
SB_HEAD_DIM = D_MODEL // SB_HEADS
SB_SCALE = 1.0 / math.sqrt(SB_HEAD_DIM)
SC_WIDTH = 3
MLA_HEADS = 8
QK_NOPE = 128
QK_ROPE = 64
V_HEAD = 128
Q_LORA = 384
KV_LORA = 256
MLA_SCALE = 1.0 / math.sqrt(QK_NOPE + QK_ROPE)
ROPE_THETA = 10000.0
N_EXPERTS = 32
TOP_K = 4
SWIGLU_LIMIT = 7.0
SWIGLU_ALPHA = 1.702
LN_EPS = 1e-5
RMS_EPS = 1e-6
DEEPNORM_ALPHA = (2 * DEPTH) ** 0.25
LOG2E = math.log2(math.e)
SB_UNDERFLOW = 110.0

LANES = 128
SUBLANES = 8
VMEM_LIMIT_BYTES = 56 * 1024 * 1024

ROW_TILE = 512
SB_TILE = 256
SB_HEADS_PER_STEP = 4
MLA_TILE = 512
MLA_HEADS_PER_STEP = 2
MOE_ROWS = 512
COMBINE_ROWS = 256
DISPATCH_ROWS = 1024
MLA_QK_PAD = 256


def _params(*semantics):
    return pltpu.CompilerParams(dimension_semantics=semantics,
                                vmem_limit_bytes=VMEM_LIMIT_BYTES)


def _deepnorm_layer_norm(h, sub, g, b):
    y = DEEPNORM_ALPHA * h + sub
    mu = jnp.mean(y, axis=-1, keepdims=True)
    d = y - mu
    var = jnp.mean(d * d, axis=-1, keepdims=True)
    return d * lax.rsqrt(var + LN_EPS) * g + b


def _proj_kernel(x_ref, w_ref, o_ref, w16_ref):
    @pl.when(pl.program_id(1) == 0)
    def _():
        w16_ref[...] = w_ref[...].astype(BF16)

    o_ref[...] = jnp.dot(x_ref[...], w16_ref[...],
                         preferred_element_type=F32).astype(o_ref.dtype)


def _proj(x16, w32, out_dtype, tn):
    n, k = x16.shape
    nout = w32.shape[1]
    tm = min(ROW_TILE, n)
    return pl.pallas_call(
        _proj_kernel,
        grid=(nout // tn, n // tm),
        in_specs=[pl.BlockSpec((tm, k), lambda j, i: (i, 0)),
                  pl.BlockSpec((k, tn), lambda j, i: (0, j))],
        out_specs=pl.BlockSpec((tm, tn), lambda j, i: (i, j)),
        out_shape=jax.ShapeDtypeStruct((n, nout), out_dtype),
        scratch_shapes=[pltpu.VMEM((k, tn), BF16)],
        compiler_params=_params("arbitrary", "arbitrary"),
        name="proj",
    )(x16, w32)


def _outproj_ln_kernel(a_ref, w_ref, h_ref, g_ref, b_ref, o32_ref, o16_ref, w16_ref):
    @pl.when(pl.program_id(0) == 0)
    def _():
        w16_ref[...] = w_ref[...].astype(BF16)

    mix = jnp.dot(a_ref[...], w16_ref[...], preferred_element_type=F32)
    out = _deepnorm_layer_norm(h_ref[...], mix, g_ref[...], b_ref[...])
    o32_ref[...] = out
    o16_ref[...] = out.astype(BF16)


def _outproj_ln(a16, w32, h32, g, b):
    n, k = a16.shape
    d = w32.shape[1]
    tm = min(ROW_TILE, n)
    return pl.pallas_call(
        _outproj_ln_kernel,
        grid=(n // tm,),
        in_specs=[pl.BlockSpec((tm, k), lambda i: (i, 0)),
                  pl.BlockSpec((k, d), lambda i: (0, 0)),
                  pl.BlockSpec((tm, d), lambda i: (i, 0)),
                  pl.BlockSpec((1, d), lambda i: (0, 0)),
                  pl.BlockSpec((1, d), lambda i: (0, 0))],
        out_specs=[pl.BlockSpec((tm, d), lambda i: (i, 0)),
                   pl.BlockSpec((tm, d), lambda i: (i, 0))],
        out_shape=[jax.ShapeDtypeStruct((n, d), F32),
                   jax.ShapeDtypeStruct((n, d), BF16)],
        scratch_shapes=[pltpu.VMEM((k, d), BF16)],
        compiler_params=_params("arbitrary"),
        name="outproj_ln",
    )(a16, w32, h32, g.reshape(1, d), b.reshape(1, d))


def _sb_kernel(q_ref, k_ref, v_ref, o_ref, acc_ref, surv_ref):
    t = SB_TILE
    i = pl.program_id(2)
    lane = lax.broadcasted_iota(jnp.int32, (t, LANES), 1)
    first_head = lane < SB_HEAD_DIM
    heads = range(SB_HEADS_PER_STEP)
    pair_lanes = lambda head: slice((head // 2) * LANES, (head // 2 + 1) * LANES)
    q_heads = []
    for head in heads:
        q = q_ref[:, pair_lanes(head)] * SB_SCALE
        keep = first_head if head % 2 == 0 else jnp.logical_not(first_head)
        q_heads.append(jnp.where(keep, q, jnp.zeros_like(q)))
    row = lax.broadcasted_iota(jnp.int32, (t, t), 0)
    col = lax.broadcasted_iota(jnp.int32, (t, t), 1)
    strictly_before = col < row
    later_key = jnp.where(row > col, 1.0, 0.0).astype(BF16)

    def scores(head, kblk):
        z = lax.dot_general(q_heads[head], kblk, (((1,), (1,)), ((), ())),
                            preferred_element_type=F32)
        sp = jnp.maximum(z, 0.0) + jnp.log(1.0 + jnp.exp(-jnp.abs(z)))
        return z, sp

    def suffix_sums(parts):
        stacked = jnp.concatenate([p.astype(BF16) for p in parts], axis=0)
        sums = jnp.dot(stacked, later_key, preferred_element_type=F32)
        return [sums[c * t:(c + 1) * t] for c in range(len(parts))]

    def diagonal(head, kblk, vblk):
        z, sp = scores(head, kblk)
        spm = jnp.where(strictly_before, sp, 0.0)
        (cs,) = suffix_sums([spm])
        a = jnp.where(strictly_before, jnp.exp(z - sp - cs), 0.0)
        acc_ref[head] = jnp.dot(a.astype(BF16), vblk, preferred_element_type=F32)
        surv_ref[head] = jnp.sum(spm, axis=1, keepdims=True)

    def diagonal_and_previous(head, kblk, vblk):
        z, sp = scores(head, kblk)
        spm = jnp.where(strictly_before, sp[:, t:], 0.0)
        cs_p, cs_d = suffix_sums([sp[:, :t], spm])
        surv_p = jnp.sum(spm, axis=1, keepdims=True)
        e = z - sp
        a = jnp.concatenate([jnp.exp(e[:, :t] - cs_p - surv_p),
                             jnp.where(strictly_before, jnp.exp(e[:, t:] - cs_d), 0.0)], axis=1)
        acc_ref[head] = jnp.dot(a.astype(BF16), vblk, preferred_element_type=F32)
        surv_ref[head] = surv_p + jnp.sum(sp[:, :t], axis=1, keepdims=True)

    def single(head, kblk, vblk):
        z, sp = scores(head, kblk)
        (cs,) = suffix_sums([sp])
        surv = surv_ref[head]
        a = jnp.exp(z - sp - cs - surv)
        acc_ref[head] += jnp.dot(a.astype(BF16), vblk, preferred_element_type=F32)
        surv_ref[head] = surv + jnp.sum(sp, axis=1, keepdims=True)

    def keep_going():
        least = jnp.min(surv_ref[0])
        for head in heads[1:]:
            least = jnp.minimum(least, jnp.min(surv_ref[head]))
        return (least < SB_UNDERFLOW).astype(jnp.int32)

    def sweep(tile_fn, start, rows):
        for head in heads:
            tile_fn(head, k_ref[pl.ds(start, rows), pair_lanes(head)],
                    v_ref[pl.ds(start, rows), pair_lanes(head)])

    @pl.when(i == 0)
    def _():
        sweep(diagonal, 0, t)

    @pl.when(i > 0)
    def _():
        sweep(diagonal_and_previous, pl.multiple_of((i - 1) * t, t), 2 * t)

    def cond(carry):
        step, go = carry
        return jnp.logical_and(step < i - 1, go > 0)

    def body(carry):
        step, _ = carry
        sweep(single, pl.multiple_of((i - 2 - step) * t, t), t)
        return step + 1, keep_going()

    lax.while_loop(cond, body, (jnp.int32(0), keep_going()))
    for head in heads[::2]:
        o_ref[:, pair_lanes(head)] = jnp.where(first_head, acc_ref[head],
                                               acc_ref[head + 1]).astype(o_ref.dtype)


def _sb_attention(qkv16, batch, seq):
    t = SB_TILE
    d = D_MODEL
    hs = SB_HEADS_PER_STEP
    width = hs * SB_HEAD_DIM
    groups = d // width
    qkv = qkv16.reshape(batch, seq, 3 * d)
    out = pl.pallas_call(
        _sb_kernel,
        grid=(batch, groups, seq // t),
        in_specs=[pl.BlockSpec((None, t, width), lambda b, p, i: (b, i, p)),
                  pl.BlockSpec((None, seq, width), lambda b, p, i: (b, 0, groups + p)),
                  pl.BlockSpec((None, seq, width), lambda b, p, i: (b, 0, 2 * groups + p))],
        out_specs=pl.BlockSpec((None, t, width), lambda b, p, i: (b, i, p)),
        out_shape=jax.ShapeDtypeStruct((batch, seq, d), BF16),
        scratch_shapes=[pltpu.VMEM((hs, t, LANES), F32), pltpu.VMEM((hs, t, 1), F32)],
        compiler_params=_params("arbitrary", "arbitrary", "arbitrary"),
        name="sb_attention",
    )(qkv, qkv, qkv)
    return out.reshape(batch * seq, d)


def _conv_kernel(gb_ref, gc_ref, hh_ref, pc_ref, ph_ref, cw_ref, w_ref, h_ref, g_ref, b_ref,
                 o32_ref, o16_ref, w16_ref, *, blocks_per_seq):
    i = pl.program_id(0)

    @pl.when(i == 0)
    def _():
        w16_ref[...] = w_ref[...].astype(BF16)

    u = gc_ref[...] * hh_ref[...]
    halo = pc_ref[...] * ph_ref[...]
    halo = jnp.where(i % blocks_per_seq == 0, jnp.zeros_like(halo), halo)
    prev1 = halo[SUBLANES - 1:SUBLANES, :]
    prev2 = halo[SUBLANES - 2:SUBLANES - 1, :]
    row = lax.broadcasted_iota(jnp.int32, u.shape, 0)
    u1 = jnp.where(row == 0, prev1, pltpu.roll(u, 1, 0))
    u2 = jnp.where(row == 0, prev2, jnp.where(row == 1, prev1, pltpu.roll(u, 2, 0)))
    conv = cw_ref[0:1, :] * u2 + cw_ref[1:2, :] * u1 + cw_ref[2:3, :] * u
    gated = (gb_ref[...] * conv).astype(BF16)
    mix = jnp.dot(gated, w16_ref[...], preferred_element_type=F32)
    out = _deepnorm_layer_norm(h_ref[...], mix, g_ref[...], b_ref[...])
    o32_ref[...] = out
    o16_ref[...] = out.astype(BF16)


def _conv_outproj_ln(proj32, conv_w, w16, h32, g, b, seq):
    n = proj32.shape[0]
    d = D_MODEL
    tm = min(ROW_TILE, seq)
    halo_blocks = tm // SUBLANES
    kern = functools.partial(_conv_kernel, blocks_per_seq=seq // tm)
    return pl.pallas_call(
        kern,
        grid=(n // tm,),
        in_specs=[pl.BlockSpec((tm, d), lambda i: (i, 0)),
                  pl.BlockSpec((tm, d), lambda i: (i, 1)),
                  pl.BlockSpec((tm, d), lambda i: (i, 2)),
                  pl.BlockSpec((SUBLANES, d), lambda i: (jnp.maximum(i * halo_blocks - 1, 0), 1)),
                  pl.BlockSpec((SUBLANES, d), lambda i: (jnp.maximum(i * halo_blocks - 1, 0), 2)),
                  pl.BlockSpec((SC_WIDTH, d), lambda i: (0, 0)),
                  pl.BlockSpec((d, d), lambda i: (0, 0)),
                  pl.BlockSpec((tm, d), lambda i: (i, 0)),
                  pl.BlockSpec((1, d), lambda i: (0, 0)),
                  pl.BlockSpec((1, d), lambda i: (0, 0))],
        out_specs=[pl.BlockSpec((tm, d), lambda i: (i, 0)),
                   pl.BlockSpec((tm, d), lambda i: (i, 0))],
        out_shape=[jax.ShapeDtypeStruct((n, d), F32),
                   jax.ShapeDtypeStruct((n, d), BF16)],
        scratch_shapes=[pltpu.VMEM((d, d), BF16)],
        compiler_params=_params("arbitrary"),
        name="conv_outproj_ln",
    )(proj32, proj32, proj32, proj32, proj32, conv_w, w16, h32, g.reshape(1, d), b.reshape(1, d))


def _rms_norm(x, g):
    ms = jnp.mean(x * x, axis=-1, keepdims=True)
    return x * lax.rsqrt(ms + RMS_EPS) * g


def _mla_prep_kernel(x_ref, pos_ref, freq_ref, sign_ref, win_ref, qn_ref, kvn_ref, wuq_ref, wukv_ref,
                     q_ref, k_ref, v_ref):
    lat = jnp.dot(x_ref[...], win_ref[...], preferred_element_type=F32)
    q_lat = lat[:, :Q_LORA]
    kv_lat = lat[:, Q_LORA:Q_LORA + KV_LORA]
    r0 = Q_LORA + KV_LORA
    k_r = lat[:, r0:r0 + QK_ROPE]
    k_r_swapped = lat[:, r0 + QK_ROPE:r0 + 2 * QK_ROPE]
    qn = _rms_norm(q_lat, qn_ref[...]).astype(BF16)
    kvn = _rms_norm(kv_lat, kvn_ref[...]).astype(BF16)
    qq = jnp.dot(qn, wuq_ref[...], preferred_element_type=F32)
    kv = jnp.dot(kvn, wukv_ref[...], preferred_element_type=F32)
    ang = pos_ref[...].astype(F32) * freq_ref[...]
    cc = jnp.cos(ang)
    ss = jnp.sin(ang) * sign_ref[...]
    k_rope = (k_r * cc + k_r_swapped * ss).astype(BF16)
    pad = jnp.zeros((x_ref.shape[0], MLA_QK_PAD - QK_NOPE - QK_ROPE), BF16)
    for h in range(MLA_HEADS):
        c = h * MLA_QK_PAD
        q_nope = qq[:, c:c + QK_NOPE]
        q_r = qq[:, c + QK_NOPE:c + QK_NOPE + QK_ROPE]
        q_r_swapped = qq[:, c + QK_NOPE + QK_ROPE:c + MLA_QK_PAD]
        q_rope = q_r * cc + q_r_swapped * ss
        q_ref[h, :, 0:QK_NOPE] = (q_nope * (MLA_SCALE * LOG2E)).astype(BF16)
        q_ref[h, :, QK_NOPE:QK_NOPE + QK_ROPE] = (q_rope * (MLA_SCALE * LOG2E)).astype(BF16)
        q_ref[h, :, QK_NOPE + QK_ROPE:MLA_QK_PAD] = pad
        c2 = h * (QK_NOPE + V_HEAD)
        k_ref[h, :, 0:QK_NOPE] = kv[:, c2:c2 + QK_NOPE].astype(BF16)
        k_ref[h, :, QK_NOPE:QK_NOPE + QK_ROPE] = k_rope
        k_ref[h, :, QK_NOPE + QK_ROPE:MLA_QK_PAD] = pad
        v_ref[h] = kv[:, c2 + QK_NOPE:c2 + QK_NOPE + V_HEAD].astype(BF16)


def _swap_halves(w):
    half = w.shape[-1] // 2
    return jnp.concatenate([w[..., half:], w[..., :half]], axis=-1)


def _mla_prep(h16, positions, w_in, q_norm, w_uq, kv_norm, w_ukv, batch, seq):
    n = h16.shape[0]
    d = D_MODEL
    tm = min(ROW_TILE, seq)
    r0 = Q_LORA + KV_LORA
    win = jnp.concatenate([w_in, _swap_halves(w_in[:, r0:r0 + QK_ROPE])], axis=1).astype(BF16)
    wq = w_uq.reshape(Q_LORA, MLA_HEADS, QK_NOPE + QK_ROPE)
    wq = jnp.concatenate([wq, _swap_halves(wq[:, :, QK_NOPE:])], axis=2)
    wq = wq.reshape(Q_LORA, MLA_HEADS * MLA_QK_PAD).astype(BF16)
    wkv = w_ukv.astype(BF16)
    inv_freq = ROPE_THETA ** (-jnp.arange(0, QK_ROPE, 2, dtype=F32) / QK_ROPE)
    freq = jnp.concatenate([inv_freq, inv_freq]).reshape(1, QK_ROPE)
    sign = jnp.concatenate([-jnp.ones((QK_ROPE // 2,), F32),
                            jnp.ones((QK_ROPE // 2,), F32)]).reshape(1, QK_ROPE)
    bps = seq // tm
    head_spec = lambda width: pl.BlockSpec((None, MLA_HEADS, tm, width),
                                           lambda i: (i // bps, 0, i % bps, 0))
    const = lambda shape: pl.BlockSpec(shape, lambda i: (0,) * len(shape))
    return pl.pallas_call(
        _mla_prep_kernel,
        grid=(n // tm,),
        in_specs=[pl.BlockSpec((tm, d), lambda i: (i, 0)),
                  pl.BlockSpec((tm, 1), lambda i: (i, 0)),
                  const((1, QK_ROPE)), const((1, QK_ROPE)),
                  const(win.shape), const((1, Q_LORA)), const((1, KV_LORA)),
                  const(wq.shape), const(wkv.shape)],
        out_specs=[head_spec(MLA_QK_PAD), head_spec(MLA_QK_PAD), head_spec(V_HEAD)],
        out_shape=[jax.ShapeDtypeStruct((batch, MLA_HEADS, seq, MLA_QK_PAD), BF16),
                   jax.ShapeDtypeStruct((batch, MLA_HEADS, seq, MLA_QK_PAD), BF16),
                   jax.ShapeDtypeStruct((batch, MLA_HEADS, seq, V_HEAD), BF16)],
        compiler_params=_params("arbitrary"),
        name="mla_prep",
    )(h16, positions.reshape(n, 1), freq, sign, win, q_norm.reshape(1, Q_LORA),
      kv_norm.reshape(1, KV_LORA), wq, wkv)


def _mla_attn_kernel(q_ref, k_ref, v_ref, o_ref, m_ref, l_ref, acc_ref, s_ref, p_ref, scale_ref):
    t = MLA_TILE
    i = pl.program_id(2)
    heads = range(MLA_HEADS_PER_STEP)
    m_ref[...] = jnp.full(m_ref.shape, -jnp.inf, F32)
    l_ref[...] = jnp.zeros(l_ref.shape, F32)
    acc_ref[...] = jnp.zeros(acc_ref.shape, F32)
    p_ref[1] = jnp.zeros(p_ref.shape[1:], BF16)
    scale_ref[1] = jnp.ones(scale_ref.shape[1:], F32)

    def scores(j, slot):
        start = pl.multiple_of(j * t, t)
        for head in heads:
            s_ref[slot, head] = lax.dot_general(
                q_ref[head], k_ref[head, pl.ds(start, t), :], (((1,), (1,)), ((), ())),
                preferred_element_type=F32)

    def accumulate(j, slot):
        start = pl.multiple_of(j * t, t)
        for head in heads:
            pv = jnp.dot(p_ref[slot, head], v_ref[head, pl.ds(start, t), :],
                         preferred_element_type=F32)
            acc_ref[head] = scale_ref[slot, head] * acc_ref[head] + pv

    def softmax(slot, diagonal):
        for head in heads:
            s = s_ref[slot, head]
            if diagonal:
                row = lax.broadcasted_iota(jnp.int32, (t, t), 0)
                col = lax.broadcasted_iota(jnp.int32, (t, t), 1)
                s = jnp.where(col <= row, s, -jnp.inf)
            m_prev = m_ref[head]
            m_new = jnp.maximum(m_prev, jnp.max(s, axis=1, keepdims=True))
            scale = jnp.exp2(m_prev - m_new)
            p = jnp.exp2(s - m_new)
            l_ref[head] = scale * l_ref[head] + jnp.sum(p, axis=1, keepdims=True)
            m_ref[head] = m_new
            scale_ref[slot, head] = scale
            p_ref[slot, head] = p.astype(BF16)

    scores(0, 0)

    def body(jj, carry):
        j = 2 * jj
        scores(j + 1, 1)
        accumulate(jnp.maximum(j - 1, 0), 1)
        softmax(0, False)
        scores(j + 2, 0)
        accumulate(j, 0)
        softmax(1, False)
        return carry

    lax.fori_loop(0, i // 2, body, 0)

    @pl.when(i % 2 == 0)
    def _():
        accumulate(jnp.maximum(i - 1, 0), 1)
        softmax(0, True)
        accumulate(i, 0)

    @pl.when(i % 2 == 1)
    def _():
        scores(i, 1)
        accumulate(jnp.maximum(i - 2, 0), 1)
        softmax(0, False)
        accumulate(i - 1, 0)
        softmax(1, True)
        accumulate(i, 1)

    for head in heads:
        o_ref[:, head * V_HEAD:(head + 1) * V_HEAD] = (
            acc_ref[head] / l_ref[head]).astype(o_ref.dtype)


def _mla_attention(q, k, v):
    batch, heads, seq, _ = q.shape
    t = MLA_TILE
    hp = MLA_HEADS_PER_STEP
    out = pl.pallas_call(
        _mla_attn_kernel,
        grid=(batch, heads // hp, seq // t),
        in_specs=[pl.BlockSpec((None, hp, t, MLA_QK_PAD), lambda b, h, i: (b, h, i, 0)),
                  pl.BlockSpec((None, hp, seq, MLA_QK_PAD), lambda b, h, i: (b, h, 0, 0)),
                  pl.BlockSpec((None, hp, seq, V_HEAD), lambda b, h, i: (b, h, 0, 0))],
        out_specs=pl.BlockSpec((None, t, hp * V_HEAD), lambda b, h, i: (b, i, h)),
        out_shape=jax.ShapeDtypeStruct((batch, seq, heads * V_HEAD), BF16),
        scratch_shapes=[pltpu.VMEM((hp, t, 1), F32), pltpu.VMEM((hp, t, 1), F32),
                        pltpu.VMEM((hp, t, V_HEAD), F32),
                        pltpu.VMEM((2, hp, t, t), F32), pltpu.VMEM((2, hp, t, t), BF16),
                        pltpu.VMEM((2, hp, t, 1), F32)],
        compiler_params=_params("arbitrary", "arbitrary", "arbitrary"),
        name="mla_attention",
    )(q, k, v)
    return out.reshape(batch * seq, heads * V_HEAD)


def _router_kernel(h_ref, whi_ref, wlo_ref, b_ref, idx_ref, gate_ref, rank_ref, count_ref, seen_ref):
    @pl.when(pl.program_id(0) == 0)
    def _():
        seen_ref[...] = jnp.zeros(seen_ref.shape, F32)

    x = h_ref[...]
    xhi = x.astype(BF16)
    xlo = (x - xhi.astype(F32)).astype(BF16)
    logits = (jnp.dot(xhi, whi_ref[...], preferred_element_type=F32)
              + jnp.dot(xhi, wlo_ref[...], preferred_element_type=F32)
              + jnp.dot(xlo, whi_ref[...], preferred_element_type=F32)) + b_ref[...]
    lane = lax.broadcasted_iota(jnp.int32, logits.shape, 1)
    lane_f = lane.astype(F32)
    vals = jnp.where(lane < N_EXPERTS, logits, -jnp.inf)
    idx_out = jnp.zeros(logits.shape, jnp.int32)
    gate_out = jnp.zeros(logits.shape, F32)
    picks = []
    top = None
    denom = None
    for k in range(TOP_K):
        m = jnp.max(vals, axis=1, keepdims=True)
        pick = jnp.min(jnp.where(vals == m, lane_f, float(LANES)), axis=1, keepdims=True)
        pick_i = pick.astype(jnp.int32)
        picks.append(pick_i)
        if k == 0:
            top = m
            e = jnp.ones_like(m)
            denom = e
        else:
            e = jnp.exp(m - top)
            denom = denom + e
        idx_out = jnp.where(lane == k, pick_i, idx_out)
        gate_out = jnp.where(lane == k, e, gate_out)
        vals = jnp.where(lane == pick_i, -jnp.inf, vals)
    idx_ref[...] = idx_out
    gate_ref[...] = gate_out / denom

    chosen = jnp.zeros(logits.shape, F32)
    for pick_i in picks:
        chosen = jnp.where(lane == pick_i, 1.0, chosen)
    tm = logits.shape[0]
    earlier = (lax.broadcasted_iota(jnp.int32, (tm, tm), 1)
               < lax.broadcasted_iota(jnp.int32, (tm, tm), 0))
    before = jnp.dot(jnp.where(earlier, 1.0, 0.0).astype(BF16), chosen.astype(BF16),
                     preferred_element_type=F32) + seen_ref[...]
    rank_out = jnp.zeros(logits.shape, jnp.int32)
    for k, pick_i in enumerate(picks):
        rank_k = jnp.sum(jnp.where(lane == pick_i, before, 0.0), axis=1, keepdims=True)
        rank_out = jnp.where(lane == k, rank_k.astype(jnp.int32), rank_out)
    rank_ref[...] = rank_out
    seen = seen_ref[...] + jnp.sum(chosen, axis=0, keepdims=True)
    seen_ref[...] = seen
    count_ref[...] = seen.astype(jnp.int32)


def _router(h32, router_w, router_b):
    n, d = h32.shape
    tm = min(ROW_TILE, n)
    w = jnp.zeros((d, LANES), F32).at[:, :N_EXPERTS].set(router_w)
    whi = w.astype(BF16)
    wlo = (w - whi.astype(F32)).astype(BF16)
    b = jnp.zeros((1, LANES), F32).at[0, :N_EXPERTS].set(router_b)
    tok = lambda: pl.BlockSpec((tm, LANES), lambda i: (i, 0))
    return pl.pallas_call(
        _router_kernel,
        grid=(n // tm,),
        in_specs=[pl.BlockSpec((tm, d), lambda i: (i, 0)),
                  pl.BlockSpec((d, LANES), lambda i: (0, 0)),
                  pl.BlockSpec((d, LANES), lambda i: (0, 0)),
                  pl.BlockSpec((1, LANES), lambda i: (0, 0))],
        out_specs=[tok(), tok(), tok(), pl.BlockSpec((1, LANES), lambda i: (0, 0))],
        out_shape=[jax.ShapeDtypeStruct((n, LANES), jnp.int32),
                   jax.ShapeDtypeStruct((n, LANES), F32),
                   jax.ShapeDtypeStruct((n, LANES), jnp.int32),
                   jax.ShapeDtypeStruct((1, LANES), jnp.int32)],
        scratch_shapes=[pltpu.VMEM((1, LANES), F32)],
        compiler_params=_params("arbitrary"),
        name="router",
    )(h32, whi, wlo, b)


def _routing_tables(top_idx, rank, counts):
    blk = MOE_ROWS
    n = top_idx.shape[0]
    n_blocks = (n * TOP_K + N_EXPERTS * blk) // blk
    padded = (counts + blk - 1) // blk * blk
    padded_end = jnp.cumsum(padded)
    padded_start = padded_end - padded
    experts = jnp.arange(N_EXPERTS, dtype=jnp.int32)
    start_of = jnp.sum(jnp.where(top_idx[..., None] == experts, padded_start, 0), axis=-1)
    dest = (start_of + rank).reshape(-1).astype(jnp.int32)
    block_row = jnp.arange(n_blocks, dtype=jnp.int32)[:, None] * blk
    block_expert = jnp.minimum(jnp.sum((padded_end[None, :] <= block_row).astype(jnp.int32), axis=1),
                               N_EXPERTS - 1).astype(jnp.int32)
    blocks_used = (padded_end[-1] // blk).astype(jnp.int32).reshape(1)
    return dest, block_expert, blocks_used


def _dispatch_kernel(dest_ref, h_ref, xs_in_hbm, xs_hbm, sem_ref):
    del xs_in_hbm
    tb = DISPATCH_ROWS
    i = pl.program_id(0)

    def issue(t, carry):
        for k in range(TOP_K):
            row = dest_ref[(i * tb + t) * TOP_K + k]
            pltpu.make_async_copy(h_ref.at[pl.ds(t, 1), :], xs_hbm.at[pl.ds(row, 1), :],
                                  sem_ref.at[0]).start()
        return carry

    lax.fori_loop(0, tb, issue, 0, unroll=8)
    for k in range(TOP_K):
        pltpu.make_async_copy(h_ref, xs_hbm.at[pl.ds(0, tb), :], sem_ref.at[0]).wait()


def _dispatch(h32, dest, xs_buffer):
    n, d = h32.shape
    n_rows = xs_buffer.shape[0]
    tb = DISPATCH_ROWS
    grid_spec = pltpu.PrefetchScalarGridSpec(
        num_scalar_prefetch=1,
        grid=(n // tb,),
        in_specs=[pl.BlockSpec((tb, d), lambda i, dest: (i, 0)),
                  pl.BlockSpec(memory_space=pl.ANY)],
        out_specs=pl.BlockSpec(memory_space=pl.ANY),
        scratch_shapes=[pltpu.SemaphoreType.DMA((1,))],
    )
    return pl.pallas_call(
        _dispatch_kernel,
        grid_spec=grid_spec,
        out_shape=jax.ShapeDtypeStruct((n_rows, d), F32),
        input_output_aliases={2: 0},
        compiler_params=_params("arbitrary"),
        name="dispatch",
    )(dest, h32, xs_buffer)


def _moe_kernel(be_ref, nb_ref, x_ref, wg_ref, bg_ref, wu_ref, bu_ref, wd_ref, bd_ref, y_ref,
                wg16_ref, wu16_ref, wd16_ref):
    b = pl.program_id(0)
    used = nb_ref[0]
    new_expert = jnp.logical_or(b == 0, be_ref[b] != be_ref[jnp.maximum(b - 1, 0)])

    @pl.when(jnp.logical_and(b < used, new_expert))
    def _():
        wg16_ref[...] = wg_ref[...].astype(BF16)
        wu16_ref[...] = wu_ref[...].astype(BF16)
        wd16_ref[...] = wd_ref[...].astype(BF16)

    @pl.when(b < used)
    def _():
        xb = x_ref[...].astype(BF16)
        g = jnp.minimum(jnp.dot(xb, wg16_ref[...], preferred_element_type=F32) + bg_ref[...],
                        SWIGLU_LIMIT)
        u = jnp.clip(jnp.dot(xb, wu16_ref[...], preferred_element_type=F32) + bu_ref[...],
                     -SWIGLU_LIMIT, SWIGLU_LIMIT)
        hdn = g * jax.nn.sigmoid(SWIGLU_ALPHA * g) * (u + 1.0)
        y_ref[...] = jnp.dot(hdn.astype(BF16), wd16_ref[...], preferred_element_type=F32) + bd_ref[...]

    @pl.when(b >= used)
    def _():
        y_ref[...] = jnp.zeros(y_ref.shape, F32)


def _moe_experts(xs, block_expert, blocks_used, layer, w_gate, b_gate, w_up, b_up, w_down, b_down):
    n_rows, d = xs.shape
    tm = MOE_ROWS
    de = w_gate.shape[3]
    expert = lambda rows, cols: pl.BlockSpec((None, None, rows, cols),
                                             lambda b, be, nb: (layer, be[b], 0, 0))
    grid_spec = pltpu.PrefetchScalarGridSpec(
        num_scalar_prefetch=2,
        grid=(n_rows // tm,),
        in_specs=[pl.BlockSpec((tm, d), lambda b, be, nb: (jnp.minimum(b, nb[0] - 1), 0)),
                  expert(d, de), expert(1, de), expert(d, de), expert(1, de),
                  expert(de, d), expert(1, d)],
        out_specs=pl.BlockSpec((tm, d), lambda b, be, nb: (b, 0)),
        scratch_shapes=[pltpu.VMEM((d, de), BF16), pltpu.VMEM((d, de), BF16),
                        pltpu.VMEM((de, d), BF16)],
    )
    return pl.pallas_call(
        _moe_kernel,
        grid_spec=grid_spec,
        out_shape=jax.ShapeDtypeStruct((n_rows, d), F32),
        compiler_params=_params("arbitrary"),
        name="moe_experts",
    )(block_expert, blocks_used, xs, w_gate, b_gate.reshape(DEPTH, N_EXPERTS, 1, de),
      w_up, b_up.reshape(DEPTH, N_EXPERTS, 1, de), w_down, b_down.reshape(DEPTH, N_EXPERTS, 1, d))


def _combine_kernel(dest_ref, y_hbm, gate_ref, h_ref, g_ref, b_ref, o32_ref, o16_ref,
                    buf_ref, sem_ref):
    tb = COMBINE_ROWS
    i = pl.program_id(0)
    steps = pl.num_programs(0)

    def start_gather(block, slot):
        def issue(t, carry):
            for k in range(TOP_K):
                row = dest_ref[(block * tb + t) * TOP_K + k]
                pltpu.make_async_copy(y_hbm.at[pl.ds(row, 1), :],
                                      buf_ref.at[slot, k, pl.ds(t, 1), :],
                                      sem_ref.at[slot]).start()
            return carry
        lax.fori_loop(0, tb, issue, 0, unroll=8)

    def wait_gather(slot):
        for k in range(TOP_K):
            pltpu.make_async_copy(y_hbm.at[pl.ds(0, tb), :], buf_ref.at[slot, k],
                                  sem_ref.at[slot]).wait()

    slot = i % 2

    @pl.when(i == 0)
    def _():
        start_gather(0, 0)

    @pl.when(i + 1 < steps)
    def _():
        start_gather(i + 1, 1 - slot)

    wait_gather(slot)
    gates = gate_ref[...]
    ffn = buf_ref[slot, 0] * gates[:, 0:1]
    for k in range(1, TOP_K):
        ffn = ffn + buf_ref[slot, k] * gates[:, k:k + 1]
    out = _deepnorm_layer_norm(h_ref[...], ffn, g_ref[...], b_ref[...])
    o32_ref[...] = out
    o16_ref[...] = out.astype(BF16)


def _combine_ln(y, dest, gates, h32, g, b):
    n, d = h32.shape
    tb = COMBINE_ROWS
    grid_spec = pltpu.PrefetchScalarGridSpec(
        num_scalar_prefetch=1,
        grid=(n // tb,),
        in_specs=[pl.BlockSpec(memory_space=pl.ANY),
                  pl.BlockSpec((tb, LANES), lambda i, dest: (i, 0)),
                  pl.BlockSpec((tb, d), lambda i, dest: (i, 0)),
                  pl.BlockSpec((1, d), lambda i, dest: (0, 0)),
                  pl.BlockSpec((1, d), lambda i, dest: (0, 0))],
        out_specs=[pl.BlockSpec((tb, d), lambda i, dest: (i, 0)),
                   pl.BlockSpec((tb, d), lambda i, dest: (i, 0))],
        scratch_shapes=[pltpu.VMEM((2, TOP_K, tb, d), F32), pltpu.SemaphoreType.DMA((2,))],
    )
    return pl.pallas_call(
        _combine_kernel,
        grid_spec=grid_spec,
        out_shape=[jax.ShapeDtypeStruct((n, d), F32), jax.ShapeDtypeStruct((n, d), BF16)],
        compiler_params=_params("arbitrary"),
        name="combine_ln",
    )(dest, y, gates, h32, g.reshape(1, d), b.reshape(1, d))


def _routed_ffn_ln(h32, layer, xs_buffer, router_w, router_b, w_gate, b_gate, w_up, b_up, w_down,
                   b_down, g, b):
    idx_pad, gate_pad, rank_pad, counts = _router(h32, router_w, router_b)
    dest, block_expert, blocks_used = _routing_tables(idx_pad[:, :TOP_K], rank_pad[:, :TOP_K],
                                                      counts[0, :N_EXPERTS])
    xs = _dispatch(h32, dest, xs_buffer)
    y = _moe_experts(xs, block_expert, blocks_used, layer,
                     w_gate, b_gate, w_up, b_up, w_down, b_down)
    h32, h16 = _combine_ln(y, dest, gate_pad, h32, g, b)
    return h32, h16, xs


def kernel(x, positions, ln_g, ln_b, sb_w_in, sb_w_out, sc_w_in, sc_conv_w, sc_w_out, mla_w_in, mla_q_norm, mla_w_uq, mla_kv_norm, mla_w_ukv, mla_w_out, router_w, router_b, moe_w_gate, moe_b_gate, moe_w_up, moe_b_up, moe_w_down, moe_b_down):
    batch, seq, d = x.shape
    n = batch * seq
    h32 = x.reshape(n, d)
    h16 = h32.astype(BF16)
    xs_buffer = jnp.zeros((n * TOP_K + N_EXPERTS * MOE_ROWS, d), F32)
    for layer in range(DEPTH):
        kind, slot = layer % N_MIXERS, layer // N_MIXERS
        g0, b0 = ln_g[layer, 0], ln_b[layer, 0]
        if kind == 0:
            qkv = _proj(h16, sb_w_in[slot], BF16, tn=d)
            attn = _sb_attention(qkv, batch, seq)
            h32, h16 = _outproj_ln(attn, sb_w_out[slot], h32, g0, b0)
        elif kind == 1:
            proj = _proj(h16, sc_w_in[slot], F32, tn=d)
            h32, h16 = _conv_outproj_ln(proj, sc_conv_w[slot], sc_w_out[slot], h32, g0, b0, seq)
        else:
            q, k, v = _mla_prep(h16, positions, mla_w_in[slot], mla_q_norm[slot], mla_w_uq[slot],
                                mla_kv_norm[slot], mla_w_ukv[slot], batch, seq)
            attn = _mla_attention(q, k, v)
            h32, h16 = _outproj_ln(attn, mla_w_out[slot], h32, g0, b0)
        h32, h16, xs_buffer = _routed_ffn_ln(h32, layer, xs_buffer, router_w[layer], router_b[layer],
                                             moe_w_gate, moe_b_gate, moe_w_up, moe_b_up, moe_w_down,
                                             moe_b_down, ln_g[layer, 1], ln_b[layer, 1])
    return h32.reshape(batch, seq, d)
```

```python
import functools
import math

import jax
import jax.numpy as jnp
from jax import lax
from jax.experimental import pallas as pl
from jax.experimental.pallas import tpu as pltpu

F32 = jnp.float32
BF16 = jnp.bfloat16

D_MODEL = 1024
DEPTH = 4
N_MIXERS = 3
SB_HEADS = 16
SB_HEAD_DIM = D_MODEL // SB_HEADS
SB_SCALE = 1.0 / math.sqrt(SB_HEAD_DIM)
SC_WIDTH = 3
MLA_HEADS = 8
QK_NOPE = 128
QK_ROPE = 64
V_HEAD = 128
Q_LORA = 384
KV_LORA = 256
MLA_SCALE = 1.0 / math.sqrt(QK_NOPE + QK_ROPE)
ROPE_THETA = 10000.0
N_EXPERTS = 32
TOP_K = 4
SWIGLU_LIMIT = 7.0
SWIGLU_ALPHA = 1.702
LN_EPS = 1e-5
RMS_EPS = 1e-6
DEEPNORM_ALPHA = (2 * DEPTH) ** 0.25
LOG2E = math.log2(math.e)
SB_UNDERFLOW = 110.0

LANES = 128
SUBLANES = 8
VMEM_LIMIT_BYTES = 56 * 1024 * 1024

ROW_TILE = 512
SB_TILE = 256
SB_HEADS_PER_STEP = 4
MLA_TILE = 512
MLA_HEADS_PER_STEP = 2
MOE_ROWS = 512
COMBINE_ROWS = 256
DISPATCH_ROWS = 1024
MLA_QK_PAD = 256


def _params(*semantics):
    return pltpu.CompilerParams(dimension_semantics=semantics,
                                vmem_limit_bytes=VMEM_LIMIT_BYTES)


def _deepnorm_layer_norm(h, sub, g, b):
    y = DEEPNORM_ALPHA * h + sub
    mu = jnp.mean(y, axis=-1, keepdims=True)
    d = y - mu
    var = jnp.mean(d * d, axis=-1, keepdims=True)
    return d * lax.rsqrt(var + LN_EPS) * g + b


def _proj_kernel(x_ref, w_ref, o_ref, w16_ref):
    @pl.when(pl.program_id(1) == 0)
    def _():
        w16_ref[...] = w_ref[...].astype(BF16)

    o_ref[...] = jnp.dot(x_ref[...], w16_ref[...],
                         preferred_element_type=F32).astype(o_ref.dtype)


def _proj(x16, w32, out_dtype, tn):
    n, k = x16.shape
    nout = w32.shape[1]
    tm = min(ROW_TILE, n)
    return pl.pallas_call(
        _proj_kernel,
        grid=(nout // tn, n // tm),
        in_specs=[pl.BlockSpec((tm, k), lambda j, i: (i, 0)),
                  pl.BlockSpec((k, tn), lambda j, i: (0, j))],
        out_specs=pl.BlockSpec((tm, tn), lambda j, i: (i, j)),
        out_shape=jax.ShapeDtypeStruct((n, nout), out_dtype),
        scratch_shapes=[pltpu.VMEM((k, tn), BF16)],
        compiler_params=_params("arbitrary", "arbitrary"),
        name="proj",
    )(x16, w32)


def _outproj_ln_kernel(a_ref, w_ref, h_ref, g_ref, b_ref, o32_ref, o16_ref, w16_ref):
    @pl.when(pl.program_id(0) == 0)
    def _():
        w16_ref[...] = w_ref[...].astype(BF16)

    mix = jnp.dot(a_ref[...], w16_ref[...], preferred_element_type=F32)
    out = _deepnorm_layer_norm(h_ref[...], mix, g_ref[...], b_ref[...])
    o32_ref[...] = out
    o16_ref[...] = out.astype(BF16)


def _outproj_ln(a16, w32, h32, g, b):
    n, k = a16.shape
    d = w32.shape[1]
    tm = min(ROW_TILE, n)
    return pl.pallas_call(
        _outproj_ln_kernel,
        grid=(n // tm,),
        in_specs=[pl.BlockSpec((tm, k), lambda i: (i, 0)),
                  pl.BlockSpec((k, d), lambda i: (0, 0)),
                  pl.BlockSpec((tm, d), lambda i: (i, 0)),
                  pl.BlockSpec((1, d), lambda i: (0, 0)),
                  pl.BlockSpec((1, d), lambda i: (0, 0))],
        out_specs=[pl.BlockSpec((tm, d), lambda i: (i, 0)),
                   pl.BlockSpec((tm, d), lambda i: (i, 0))],
        out_shape=[jax.ShapeDtypeStruct((n, d), F32),
                   jax.ShapeDtypeStruct((n, d), BF16)],
        scratch_shapes=[pltpu.VMEM((k, d), BF16)],
        compiler_params=_params("arbitrary"),
        name="outproj_ln",
    )(a16, w32, h32, g.reshape(1, d), b.reshape(1, d))


def _sb_kernel(q_ref, k_ref, v_ref, o_ref, acc_ref, surv_ref):
    t = SB_TILE
    i = pl.program_id(2)
    lane = lax.broadcasted_iota(jnp.int32, (t, LANES), 1)
    first_head = lane < SB_HEAD_DIM
    heads = range(SB_HEADS_PER_STEP)
    pair_lanes = lambda head: slice((head // 2) * LANES, (head // 2 + 1) * LANES)
    q_heads = []
    for head in heads:
        q = q_ref[:, pair_lanes(head)] * SB_SCALE
        keep = first_head if head % 2 == 0 else jnp.logical_not(first_head)
        q_heads.append(jnp.where(keep, q, jnp.zeros_like(q)))
    row = lax.broadcasted_iota(jnp.int32, (t, t), 0)
    col = lax.broadcasted_iota(jnp.int32, (t, t), 1)
    strictly_before = col < row
    later_key = jnp.where(row > col, 1.0, 0.0).astype(BF16)

    def scores(head, kblk):
        z = lax.dot_general(q_heads[head], kblk, (((1,), (1,)), ((), ())),
                            preferred_element_type=F32)
        sp = jnp.maximum(z, 0.0) + jnp.log(1.0 + jnp.exp(-jnp.abs(z)))
        return z, sp

    def suffix_sums(parts):
        stacked = jnp.concatenate([p.astype(BF16) for p in parts], axis=0)
        sums = jnp.dot(stacked, later_key, preferred_element_type=F32)
        return [sums[c * t:(c + 1) * t] for c in range(len(parts))]

    def diagonal(head, kblk, vblk):
        z, sp = scores(head, kblk)
        spm = jnp.where(strictly_before, sp, 0.0)
        (cs,) = suffix_sums([spm])
        a = jnp.where(strictly_before, jnp.exp(z - sp - cs), 0.0)
        acc_ref[head] = jnp.dot(a.astype(BF16), vblk, preferred_element_type=F32)
        surv_ref[head] = jnp.sum(spm, axis=1, keepdims=True)

    def diagonal_and_previous(head, kblk, vblk):
        z, sp = scores(head, kblk)
        spm = jnp.where(strictly_before, sp[:, t:], 0.0)
        cs_p, cs_d = suffix_sums([sp[:, :t], spm])
        surv_p = jnp.sum(spm, axis=1, keepdims=True)
        e = z - sp
        a = jnp.concatenate([jnp.exp(e[:, :t] - cs_p - surv_p),
                             jnp.where(strictly_before, jnp.exp(e[:, t:] - cs_d), 0.0)], axis=1)
        acc_ref[head] = jnp.dot(a.astype(BF16), vblk, preferred_element_type=F32)
        surv_ref[head] = surv_p + jnp.sum(sp[:, :t], axis=1, keepdims=True)

    def single(head, kblk, vblk):
        z, sp = scores(head, kblk)
        (cs,) = suffix_sums([sp])
        surv = surv_ref[head]
        a = jnp.exp(z - sp - cs - surv)
        acc_ref[head] += jnp.dot(a.astype(BF16), vblk, preferred_element_type=F32)
        surv_ref[head] = surv + jnp.sum(sp, axis=1, keepdims=True)

    def keep_going():
        least = jnp.min(surv_ref[0])
        for head in heads[1:]:
            least = jnp.minimum(least, jnp.min(surv_ref[head]))
        return (least < SB_UNDERFLOW).astype(jnp.int32)

    def sweep(tile_fn, start, rows):
        for head in heads:
            tile_fn(head, k_ref[pl.ds(start, rows), pair_lanes(head)],
                    v_ref[pl.ds(start, rows), pair_lanes(head)])

    @pl.when(i == 0)
    def _():
        sweep(diagonal, 0, t)

    @pl.when(i > 0)
    def _():
        sweep(diagonal_and_previous, pl.multiple_of((i - 1) * t, t), 2 * t)

    def cond(carry):
        step, go = carry
        return jnp.logical_and(step < i - 1, go > 0)

    def body(carry):
        step, _ = carry
        sweep(single, pl.multiple_of((i - 2 - step) * t, t), t)
        return step + 1, keep_going()

    lax.while_loop(cond, body, (jnp.int32(0), keep_going()))
    for head in heads[::2]:
        o_ref[:, pair_lanes(head)] = jnp.where(first_head, acc_ref[head],
                                               acc_ref[head + 1]).astype(o_ref.dtype)


def _sb_attention(qkv16, batch, seq):
    t = SB_TILE
    d = D_MODEL
    hs = SB_HEADS_PER_STEP
    width = hs * SB_HEAD_DIM
    groups = d // width
    qkv = qkv16.reshape(batch, seq, 3 * d)
    out = pl.pallas_call(
        _sb_kernel,
        grid=(batch, groups, seq // t),
        in_specs=[pl.BlockSpec((None, t, width), lambda b, p, i: (b, i, p)),
                  pl.BlockSpec((None, seq, width), lambda b, p, i: (b, 0, groups + p)),
                  pl.BlockSpec((None, seq, width), lambda b, p, i: (b, 0, 2 * groups + p))],
        out_specs=pl.BlockSpec((None, t, width), lambda b, p, i: (b, i, p)),
        out_shape=jax.ShapeDtypeStruct((batch, seq, d), BF16),
        scratch_shapes=[pltpu.VMEM((hs, t, LANES), F32), pltpu.VMEM((hs, t, 1), F32)],
        compiler_params=_params("arbitrary", "arbitrary", "arbitrary"),
        name="sb_attention",
    )(qkv, qkv, qkv)
    return out.reshape(batch * seq, d)


def _conv_kernel(gb_ref, gc_ref, hh_ref, pc_ref, ph_ref, cw_ref, w_ref, h_ref, g_ref, b_ref,
                 o32_ref, o16_ref, w16_ref, *, blocks_per_seq):
    i = pl.program_id(0)

    @pl.when(i == 0)
    def _():
        w16_ref[...] = w_ref[...].astype(BF16)

    u = gc_ref[...] * hh_ref[...]
    halo = pc_ref[...] * ph_ref[...]
    halo = jnp.where(i % blocks_per_seq == 0, jnp.zeros_like(halo), halo)
    prev1 = halo[SUBLANES - 1:SUBLANES, :]
    prev2 = halo[SUBLANES - 2:SUBLANES - 1, :]
    row = lax.broadcasted_iota(jnp.int32, u.shape, 0)
    u1 = jnp.where(row == 0, prev1, pltpu.roll(u, 1, 0))
    u2 = jnp.where(row == 0, prev2, jnp.where(row == 1, prev1, pltpu.roll(u, 2, 0)))
    conv = cw_ref[0:1, :] * u2 + cw_ref[1:2, :] * u1 + cw_ref[2:3, :] * u
    gated = (gb_ref[...] * conv).astype(BF16)
    mix = jnp.dot(gated, w16_ref[...], preferred_element_type=F32)
    out = _deepnorm_layer_norm(h_ref[...], mix, g_ref[...], b_ref[...])
    o32_ref[...] = out
    o16_ref[...] = out.astype(BF16)


def _conv_outproj_ln(proj32, conv_w, w16, h32, g, b, seq):
    n = proj32.shape[0]
    d = D_MODEL
    tm = min(ROW_TILE, seq)
    halo_blocks = tm // SUBLANES
    kern = functools.partial(_conv_kernel, blocks_per_seq=seq // tm)
    return pl.pallas_call(
        kern,
        grid=(n // tm,),
        in_specs=[pl.BlockSpec((tm, d), lambda i: (i, 0)),
                  pl.BlockSpec((tm, d), lambda i: (i, 1)),
                  pl.BlockSpec((tm, d), lambda i: (i, 2)),
                  pl.BlockSpec((SUBLANES, d), lambda i: (jnp.maximum(i * halo_blocks - 1, 0), 1)),
                  pl.BlockSpec((SUBLANES, d), lambda i: (jnp.maximum(i * halo_blocks - 1, 0), 2)),
                  pl.BlockSpec((SC_WIDTH, d), lambda i: (0, 0)),
                  pl.BlockSpec((d, d), lambda i: (0, 0)),
                  pl.BlockSpec((tm, d), lambda i: (i, 0)),
                  pl.BlockSpec((1, d), lambda i: (0, 0)),
                  pl.BlockSpec((1, d), lambda i: (0, 0))],
        out_specs=[pl.BlockSpec((tm, d), lambda i: (i, 0)),
                   pl.BlockSpec((tm, d), lambda i: (i, 0))],
        out_shape=[jax.ShapeDtypeStruct((n, d), F32),
                   jax.ShapeDtypeStruct((n, d), BF16)],
        scratch_shapes=[pltpu.VMEM((d, d), BF16)],
        compiler_params=_params("arbitrary"),
        name="conv_outproj_ln",
    )(proj32, proj32, proj32, proj32, proj32, conv_w, w16, h32, g.reshape(1, d), b.reshape(1, d))


def _rms_norm(x, g):
    ms = jnp.mean(x * x, axis=-1, keepdims=True)
    return x * lax.rsqrt(ms + RMS_EPS) * g


def _mla_prep_kernel(x_ref, pos_ref, freq_ref, sign_ref, win_ref, qn_ref, kvn_ref, wuq_ref, wukv_ref,
                     q_ref, k_ref, v_ref):
    lat = jnp.dot(x_ref[...], win_ref[...], preferred_element_type=F32)
    q_lat = lat[:, :Q_LORA]
    kv_lat = lat[:, Q_LORA:Q_LORA + KV_LORA]
    r0 = Q_LORA + KV_LORA
    k_r = lat[:, r0:r0 + QK_ROPE]
    k_r_swapped = lat[:, r0 + QK_ROPE:r0 + 2 * QK_ROPE]
    qn = _rms_norm(q_lat, qn_ref[...]).astype(BF16)
    kvn = _rms_norm(kv_lat, kvn_ref[...]).astype(BF16)
    qq = jnp.dot(qn, wuq_ref[...], preferred_element_type=F32)
    kv = jnp.dot(kvn, wukv_ref[...], preferred_element_type=F32)
    ang = pos_ref[...].astype(F32) * freq_ref[...]
    cc = jnp.cos(ang)
    ss = jnp.sin(ang) * sign_ref[...]
    k_rope = (k_r * cc + k_r_swapped * ss).astype(BF16)
    pad = jnp.zeros((x_ref.shape[0], MLA_QK_PAD - QK_NOPE - QK_ROPE), BF16)
    for h in range(MLA_HEADS):
        c = h * MLA_QK_PAD
        q_nope = qq[:, c:c + QK_NOPE]
        q_r = qq[:, c + QK_NOPE:c + QK_NOPE + QK_ROPE]
        q_r_swapped = qq[:, c + QK_NOPE + QK_ROPE:c + MLA_QK_PAD]
        q_rope = q_r * cc + q_r_swapped * ss
        q_ref[h, :, 0:QK_NOPE] = (q_nope * (MLA_SCALE * LOG2E)).astype(BF16)
        q_ref[h, :, QK_NOPE:QK_NOPE + QK_ROPE] = (q_rope * (MLA_SCALE * LOG2E)).astype(BF16)
        q_ref[h, :, QK_NOPE + QK_ROPE:MLA_QK_PAD] = pad
        c2 = h * (QK_NOPE + V_HEAD)
        k_ref[h, :, 0:QK_NOPE] = kv[:, c2:c2 + QK_NOPE].astype(BF16)
        k_ref[h, :, QK_NOPE:QK_NOPE + QK_ROPE] = k_rope
        k_ref[h, :, QK_NOPE + QK_ROPE:MLA_QK_PAD] = pad
        v_ref[h] = kv[:, c2 + QK_NOPE:c2 + QK_NOPE + V_HEAD].astype(BF16)


def _swap_halves(w):
    half = w.shape[-1] // 2
    return jnp.concatenate([w[..., half:], w[..., :half]], axis=-1)


def _mla_prep(h16, positions, w_in, q_norm, w_uq, kv_norm, w_ukv, batch, seq):
    n = h16.shape[0]
    d = D_MODEL
    tm = min(ROW_TILE, seq)
    r0 = Q_LORA + KV_LORA
    win = jnp.concatenate([w_in, _swap_halves(w_in[:, r0:r0 + QK_ROPE])], axis=1).astype(BF16)
    wq = w_uq.reshape(Q_LORA, MLA_HEADS, QK_NOPE + QK_ROPE)
    wq = jnp.concatenate([wq, _swap_halves(wq[:, :, QK_NOPE:])], axis=2)
    wq = wq.reshape(Q_LORA, MLA_HEADS * MLA_QK_PAD).astype(BF16)
    wkv = w_ukv.astype(BF16)
    inv_freq = ROPE_THETA ** (-jnp.arange(0, QK_ROPE, 2, dtype=F32) / QK_ROPE)
    freq = jnp.concatenate([inv_freq, inv_freq]).reshape(1, QK_ROPE)
    sign = jnp.concatenate([-jnp.ones((QK_ROPE // 2,), F32),
                            jnp.ones((QK_ROPE // 2,), F32)]).reshape(1, QK_ROPE)
    bps = seq // tm
    head_spec = lambda width: pl.BlockSpec((None, MLA_HEADS, tm, width),
                                           lambda i: (i // bps, 0, i % bps, 0))
    const = lambda shape: pl.BlockSpec(shape, lambda i: (0,) * len(shape))
    return pl.pallas_call(
        _mla_prep_kernel,
        grid=(n // tm,),
        in_specs=[pl.BlockSpec((tm, d), lambda i: (i, 0)),
                  pl.BlockSpec((tm, 1), lambda i: (i, 0)),
                  const((1, QK_ROPE)), const((1, QK_ROPE)),
                  const(win.shape), const((1, Q_LORA)), const((1, KV_LORA)),
                  const(wq.shape), const(wkv.shape)],
        out_specs=[head_spec(MLA_QK_PAD), head_spec(MLA_QK_PAD), head_spec(V_HEAD)],
        out_shape=[jax.ShapeDtypeStruct((batch, MLA_HEADS, seq, MLA_QK_PAD), BF16),
                   jax.ShapeDtypeStruct((batch, MLA_HEADS, seq, MLA_QK_PAD), BF16),
                   jax.ShapeDtypeStruct((batch, MLA_HEADS, seq, V_HEAD), BF16)],
        compiler_params=_params("arbitrary"),
        name="mla_prep",
    )(h16, positions.reshape(n, 1), freq, sign, win, q_norm.reshape(1, Q_LORA),
      kv_norm.reshape(1, KV_LORA), wq, wkv)


def _mla_attn_kernel(q_ref, k_ref, v_ref, o_ref, m_ref, l_ref, acc_ref, s_ref, p_ref, scale_ref):
    t = MLA_TILE
    i = pl.program_id(2)
    heads = range(MLA_HEADS_PER_STEP)
    m_ref[...] = jnp.full(m_ref.shape, -jnp.inf, F32)
    l_ref[...] = jnp.zeros(l_ref.shape, F32)
    acc_ref[...] = jnp.zeros(acc_ref.shape, F32)
    p_ref[1] = jnp.zeros(p_ref.shape[1:], BF16)
    scale_ref[1] = jnp.ones(scale_ref.shape[1:], F32)

    def scores(j, slot):
        start = pl.multiple_of(j * t, t)
        for head in heads:
            s_ref[slot, head] = lax.dot_general(
                q_ref[head], k_ref[head, pl.ds(start, t), :], (((1,), (1,)), ((), ())),
                preferred_element_type=F32)

    def accumulate(j, slot):
        start = pl.multiple_of(j * t, t)
        for head in heads:
            pv = jnp.dot(p_ref[slot, head], v_ref[head, pl.ds(start, t), :],
                         preferred_element_type=F32)
            acc_ref[head] = scale_ref[slot, head] * acc_ref[head] + pv

    def softmax(slot, diagonal):
        for head in heads:
            s = s_ref[slot, head]
            if diagonal:
                row = lax.broadcasted_iota(jnp.int32, (t, t), 0)
                col = lax.broadcasted_iota(jnp.int32, (t, t), 1)
                s = jnp.where(col <= row, s, -jnp.inf)
            m_prev = m_ref[head]
            m_new = jnp.maximum(m_prev, jnp.max(s, axis=1, keepdims=True))
            scale = jnp.exp2(m_prev - m_new)
            p = jnp.exp2(s - m_new)
            l_ref[head] = scale * l_ref[head] + jnp.sum(p, axis=1, keepdims=True)
            m_ref[head] = m_new
            scale_ref[slot, head] = scale
            p_ref[slot, head] = p.astype(BF16)

    scores(0, 0)

    def body(jj, carry):
        j = 2 * jj
        scores(j + 1, 1)
        accumulate(jnp.maximum(j - 1, 0), 1)
        softmax(0, False)
        scores(j + 2, 0)
        accumulate(j, 0)
        softmax(1, False)
        return carry

    lax.fori_loop(0, i // 2, body, 0)

    @pl.when(i % 2 == 0)
    def _():
        accumulate(jnp.maximum(i - 1, 0), 1)
        softmax(0, True)
        accumulate(i, 0)

    @pl.when(i % 2 == 1)
    def _():
        scores(i, 1)
        accumulate(jnp.maximum(i - 2, 0), 1)
        softmax(0, False)
        accumulate(i - 1, 0)
        softmax(1, True)
        accumulate(i, 1)

    for head in heads:
        o_ref[:, head * V_HEAD:(head + 1) * V_HEAD] = (
            acc_ref[head] / l_ref[head]).astype(o_ref.dtype)


def _mla_attention(q, k, v):
    batch, heads, seq, _ = q.shape
    t = MLA_TILE
    hp = MLA_HEADS_PER_STEP
    out = pl.pallas_call(
        _mla_attn_kernel,
        grid=(batch, heads // hp, seq // t),
        in_specs=[pl.BlockSpec((None, hp, t, MLA_QK_PAD), lambda b, h, i: (b, h, i, 0)),
                  pl.BlockSpec((None, hp, seq, MLA_QK_PAD), lambda b, h, i: (b, h, 0, 0)),
                  pl.BlockSpec((None, hp, seq, V_HEAD), lambda b, h, i: (b, h, 0, 0))],
        out_specs=pl.BlockSpec((None, t, hp * V_HEAD), lambda b, h, i: (b, i, h)),
        out_shape=jax.ShapeDtypeStruct((batch, seq, heads * V_HEAD), BF16),
        scratch_shapes=[pltpu.VMEM((hp, t, 1), F32), pltpu.VMEM((hp, t, 1), F32),
                        pltpu.VMEM((hp, t, V_HEAD), F32),
                        pltpu.VMEM((2, hp, t, t), F32), pltpu.VMEM((2, hp, t, t), BF16),
                        pltpu.VMEM((2, hp, t, 1), F32)],
        compiler_params=_params("arbitrary", "arbitrary", "arbitrary"),
        name="mla_attention",
    )(q, k, v)
    return out.reshape(batch * seq, heads * V_HEAD)


def _router_kernel(h_ref, whi_ref, wlo_ref, b_ref, idx_ref, gate_ref, rank_ref, count_ref, seen_ref):
    @pl.when(pl.program_id(0) == 0)
    def _():
        seen_ref[...] = jnp.zeros(seen_ref.shape, F32)

    x = h_ref[...]
    xhi = x.astype(BF16)
    xlo = (x - xhi.astype(F32)).astype(BF16)
    logits = (jnp.dot(xhi, whi_ref[...], preferred_element_type=F32)
              + jnp.dot(xhi, wlo_ref[...], preferred_element_type=F32)
              + jnp.dot(xlo, whi_ref[...], preferred_element_type=F32)) + b_ref[...]
    lane = lax.broadcasted_iota(jnp.int32, logits.shape, 1)
    lane_f = lane.astype(F32)
    vals = jnp.where(lane < N_EXPERTS, logits, -jnp.inf)
    idx_out = jnp.zeros(logits.shape, jnp.int32)
    gate_out = jnp.zeros(logits.shape, F32)
    picks = []
    top = None
    denom = None
    for k in range(TOP_K):
        m = jnp.max(vals, axis=1, keepdims=True)
        pick = jnp.min(jnp.where(vals == m, lane_f, float(LANES)), axis=1, keepdims=True)
        pick_i = pick.astype(jnp.int32)
        picks.append(pick_i)
        if k == 0:
            top = m
            e = jnp.ones_like(m)
            denom = e
        else:
            e = jnp.exp(m - top)
            denom = denom + e
        idx_out = jnp.where(lane == k, pick_i, idx_out)
        gate_out = jnp.where(lane == k, e, gate_out)
        vals = jnp.where(lane == pick_i, -jnp.inf, vals)
    idx_ref[...] = idx_out
    gate_ref[...] = gate_out / denom

    chosen = jnp.zeros(logits.shape, F32)
    for pick_i in picks:
        chosen = jnp.where(lane == pick_i, 1.0, chosen)
    tm = logits.shape[0]
    earlier = (lax.broadcasted_iota(jnp.int32, (tm, tm), 1)
               < lax.broadcasted_iota(jnp.int32, (tm, tm), 0))
    before = jnp.dot(jnp.where(earlier, 1.0, 0.0).astype(BF16), chosen.astype(BF16),
                     preferred_element_type=F32) + seen_ref[...]
    rank_out = jnp.zeros(logits.shape, jnp.int32)
    for k, pick_i in enumerate(picks):
        rank_k = jnp.sum(jnp.where(lane == pick_i, before, 0.0), axis=1, keepdims=True)
        rank_out = jnp.where(lane == k, rank_k.astype(jnp.int32), rank_out)
    rank_ref[...] = rank_out
    seen = seen_ref[...] + jnp.sum(chosen, axis=0, keepdims=True)
    seen_ref[...] = seen
    count_ref[...] = seen.astype(jnp.int32)


def _router(h32, router_w, router_b):
    n, d = h32.shape
    tm = min(ROW_TILE, n)
    w = jnp.zeros((d, LANES), F32).at[:, :N_EXPERTS].set(router_w)
    whi = w.astype(BF16)
    wlo = (w - whi.astype(F32)).astype(BF16)
    b = jnp.zeros((1, LANES), F32).at[0, :N_EXPERTS].set(router_b)
    tok = lambda: pl.BlockSpec((tm, LANES), lambda i: (i, 0))
    return pl.pallas_call(
        _router_kernel,
        grid=(n // tm,),
        in_specs=[pl.BlockSpec((tm, d), lambda i: (i, 0)),
                  pl.BlockSpec((d, LANES), lambda i: (0, 0)),
                  pl.BlockSpec((d, LANES), lambda i: (0, 0)),
                  pl.BlockSpec((1, LANES), lambda i: (0, 0))],
        out_specs=[tok(), tok(), tok(), pl.BlockSpec((1, LANES), lambda i: (0, 0))],
        out_shape=[jax.ShapeDtypeStruct((n, LANES), jnp.int32),
                   jax.ShapeDtypeStruct((n, LANES), F32),
                   jax.ShapeDtypeStruct((n, LANES), jnp.int32),
                   jax.ShapeDtypeStruct((1, LANES), jnp.int32)],
        scratch_shapes=[pltpu.VMEM((1, LANES), F32)],
        compiler_params=_params("arbitrary"),
        name="router",
    )(h32, whi, wlo, b)


def _routing_tables(top_idx, rank, counts):
    blk = MOE_ROWS
    n = top_idx.shape[0]
    n_blocks = (n * TOP_K + N_EXPERTS * blk) // blk
    padded = (counts + blk - 1) // blk * blk
    padded_end = jnp.cumsum(padded)
    padded_start = padded_end - padded
    experts = jnp.arange(N_EXPERTS, dtype=jnp.int32)
    start_of = jnp.sum(jnp.where(top_idx[..., None] == experts, padded_start, 0), axis=-1)
    dest = (start_of + rank).reshape(-1).astype(jnp.int32)
    block_row = jnp.arange(n_blocks, dtype=jnp.int32)[:, None] * blk
    block_expert = jnp.minimum(jnp.sum((padded_end[None, :] <= block_row).astype(jnp.int32), axis=1),
                               N_EXPERTS - 1).astype(jnp.int32)
    blocks_used = (padded_end[-1] // blk).astype(jnp.int32).reshape(1)
    return dest, block_expert, blocks_used


def _dispatch_kernel(dest_ref, h_ref, xs_in_hbm, xs_hbm, sem_ref):
    del xs_in_hbm
    tb = DISPATCH_ROWS
    i = pl.program_id(0)

    def issue(t, carry):
        for k in range(TOP_K):
            row = dest_ref[(i * tb + t) * TOP_K + k]
            pltpu.make_async_copy(h_ref.at[pl.ds(t, 1), :], xs_hbm.at[pl.ds(row, 1), :],
                                  sem_ref.at[0]).start(priority=k % 2)
        return carry

    lax.fori_loop(0, tb, issue, 0, unroll=8)
    for k in range(TOP_K):
        pltpu.make_async_copy(h_ref, xs_hbm.at[pl.ds(0, tb), :], sem_ref.at[0]).wait()


def _dispatch(h32, dest, xs_buffer):
    n, d = h32.shape
    n_rows = xs_buffer.shape[0]
    tb = DISPATCH_ROWS
    grid_spec = pltpu.PrefetchScalarGridSpec(
        num_scalar_prefetch=1,
        grid=(n // tb,),
        in_specs=[pl.BlockSpec((tb, d), lambda i, dest: (i, 0)),
                  pl.BlockSpec(memory_space=pl.ANY)],
        out_specs=pl.BlockSpec(memory_space=pl.ANY),
        scratch_shapes=[pltpu.SemaphoreType.DMA((1,))],
    )
    return pl.pallas_call(
        _dispatch_kernel,
        grid_spec=grid_spec,
        out_shape=jax.ShapeDtypeStruct((n_rows, d), F32),
        input_output_aliases={2: 0},
        compiler_params=_params("arbitrary"),
        name="dispatch",
    )(dest, h32, xs_buffer)


def _moe_kernel(be_ref, nb_ref, x_ref, wg_ref, bg_ref, wu_ref, bu_ref, wd_ref, bd_ref, y_ref,
                wg16_ref, wu16_ref, wd16_ref):
    b = pl.program_id(0)
    used = nb_ref[0]
    new_expert = jnp.logical_or(b == 0, be_ref[b] != be_ref[jnp.maximum(b - 1, 0)])

    @pl.when(jnp.logical_and(b < used, new_expert))
    def _():
        wg16_ref[...] = wg_ref[...].astype(BF16)
        wu16_ref[...] = wu_ref[...].astype(BF16)
        wd16_ref[...] = wd_ref[...].astype(BF16)

    @pl.when(b < used)
    def _():
        xb = x_ref[...].astype(BF16)
        g = jnp.minimum(jnp.dot(xb, wg16_ref[...], preferred_element_type=F32) + bg_ref[...],
                        SWIGLU_LIMIT)
        u = jnp.clip(jnp.dot(xb, wu16_ref[...], preferred_element_type=F32) + bu_ref[...],
                     -SWIGLU_LIMIT, SWIGLU_LIMIT)
        hdn = g * jax.nn.sigmoid(SWIGLU_ALPHA * g) * (u + 1.0)
        y_ref[...] = jnp.dot(hdn.astype(BF16), wd16_ref[...], preferred_element_type=F32) + bd_ref[...]

    @pl.when(b >= used)
    def _():
        y_ref[...] = jnp.zeros(y_ref.shape, F32)


def _moe_experts(xs, block_expert, blocks_used, layer, w_gate, b_gate, w_up, b_up, w_down, b_down):
    n_rows, d = xs.shape
    tm = MOE_ROWS
    de = w_gate.shape[3]
    expert = lambda rows, cols: pl.BlockSpec((None, None, rows, cols),
                                             lambda b, be, nb: (layer, be[b], 0, 0))
    grid_spec = pltpu.PrefetchScalarGridSpec(
        num_scalar_prefetch=2,
        grid=(n_rows // tm,),
        in_specs=[pl.BlockSpec((tm, d), lambda b, be, nb: (jnp.minimum(b, nb[0] - 1), 0)),
                  expert(d, de), expert(1, de), expert(d, de), expert(1, de),
                  expert(de, d), expert(1, d)],
        out_specs=pl.BlockSpec((tm, d), lambda b, be, nb: (b, 0)),
        scratch_shapes=[pltpu.VMEM((d, de), BF16), pltpu.VMEM((d, de), BF16),
                        pltpu.VMEM((de, d), BF16)],
    )
    return pl.pallas_call(
        _moe_kernel,
        grid_spec=grid_spec,
        out_shape=jax.ShapeDtypeStruct((n_rows, d), F32),
        compiler_params=_params("arbitrary"),
        name="moe_experts",
    )(block_expert, blocks_used, xs, w_gate, b_gate.reshape(DEPTH, N_EXPERTS, 1, de),
      w_up, b_up.reshape(DEPTH, N_EXPERTS, 1, de), w_down, b_down.reshape(DEPTH, N_EXPERTS, 1, d))


def _combine_kernel(dest_ref, y_hbm, gate_ref, h_ref, g_ref, b_ref, o32_ref, o16_ref,
                    buf_ref, sem_ref):
    tb = COMBINE_ROWS
    i = pl.program_id(0)
    steps = pl.num_programs(0)

    def start_gather(block, slot):
        def issue(t, carry):
            for k in range(TOP_K):
                row = dest_ref[(block * tb + t) * TOP_K + k]
                pltpu.make_async_copy(y_hbm.at[pl.ds(row, 1), :],
                                      buf_ref.at[slot, k, pl.ds(t, 1), :],
                                      sem_ref.at[slot]).start(priority=k % 2)
            return carry
        lax.fori_loop(0, tb, issue, 0, unroll=8)

    def wait_gather(slot):
        for k in range(TOP_K):
            pltpu.make_async_copy(y_hbm.at[pl.ds(0, tb), :], buf_ref.at[slot, k],
                                  sem_ref.at[slot]).wait()

    slot = i % 2

    @pl.when(i == 0)
    def _():
        start_gather(0, 0)

    @pl.when(i + 1 < steps)
    def _():
        start_gather(i + 1, 1 - slot)

    wait_gather(slot)
    gates = gate_ref[...]
    ffn = buf_ref[slot, 0] * gates[:, 0:1]
    for k in range(1, TOP_K):
        ffn = ffn + buf_ref[slot, k] * gates[:, k:k + 1]
    out = _deepnorm_layer_norm(h_ref[...], ffn, g_ref[...], b_ref[...])
    o32_ref[...] = out
    o16_ref[...] = out.astype(BF16)


def _combine_ln(y, dest, gates, h32, g, b):
    n, d = h32.shape
    tb = COMBINE_ROWS
    grid_spec = pltpu.PrefetchScalarGridSpec(
        num_scalar_prefetch=1,
        grid=(n // tb,),
        in_specs=[pl.BlockSpec(memory_space=pl.ANY),
                  pl.BlockSpec((tb, LANES), lambda i, dest: (i, 0)),
                  pl.BlockSpec((tb, d), lambda i, dest: (i, 0)),
                  pl.BlockSpec((1, d), lambda i, dest: (0, 0)),
                  pl.BlockSpec((1, d), lambda i, dest: (0, 0))],
        out_specs=[pl.BlockSpec((tb, d), lambda i, dest: (i, 0)),
                   pl.BlockSpec((tb, d), lambda i, dest: (i, 0))],
        scratch_shapes=[pltpu.VMEM((2, TOP_K, tb, d), F32), pltpu.SemaphoreType.DMA((2,))],
    )
    return pl.pallas_call(
        _combine_kernel,
        grid_spec=grid_spec,
        out_shape=[jax.ShapeDtypeStruct((n, d), F32), jax.ShapeDtypeStruct((n, d), BF16)],
        compiler_params=_params("arbitrary"),
        name="combine_ln",
    )(dest, y, gates, h32, g.reshape(1, d), b.reshape(1, d))


def _routed_ffn_ln(h32, layer, xs_buffer, router_w, router_b, w_gate, b_gate, w_up, b_up, w_down,
                   b_down, g, b):
    idx_pad, gate_pad, rank_pad, counts = _router(h32, router_w, router_b)
    dest, block_expert, blocks_used = _routing_tables(idx_pad[:, :TOP_K], rank_pad[:, :TOP_K],
                                                      counts[0, :N_EXPERTS])
    xs = _dispatch(h32, dest, xs_buffer)
    y = _moe_experts(xs, block_expert, blocks_used, layer,
                     w_gate, b_gate, w_up, b_up, w_down, b_down)
    h32, h16 = _combine_ln(y, dest, gate_pad, h32, g, b)
    return h32, h16, xs


def kernel(x, positions, ln_g, ln_b, sb_w_in, sb_w_out, sc_w_in, sc_conv_w, sc_w_out, mla_w_in, mla_q_norm, mla_w_uq, mla_kv_norm, mla_w_ukv, mla_w_out, router_w, router_b, moe_w_gate, moe_b_gate, moe_w_up, moe_b_up, moe_w_down, moe_b_down):
    batch, seq, d = x.shape
    n = batch * seq
    h32 = x.reshape(n, d)
    h16 = h32.astype(BF16)
    xs_buffer = jnp.zeros((n * TOP_K + N_EXPERTS * MOE_ROWS, d), F32)
    for layer in range(DEPTH):
        kind, slot = layer % N_MIXERS, layer // N_MIXERS
        g0, b0 = ln_g[layer, 0], ln_b[layer, 0]
        if kind == 0:
            qkv = _proj(h16, sb_w_in[slot], BF16, tn=d)
            attn = _sb_attention(qkv, batch, seq)
            h32, h16 = _outproj_ln(attn, sb_w_out[slot], h32, g0, b0)
        elif kind == 1:
            proj = _proj(h16, sc_w_in[slot], F32, tn=d)
            h32, h16 = _conv_outproj_ln(proj, sc_conv_w[slot], sc_w_out[slot], h32, g0, b0, seq)
        else:
            q, k, v = _mla_prep(h16, positions, mla_w_in[slot], mla_q_norm[slot], mla_w_uq[slot],
                                mla_kv_norm[slot], mla_w_ukv[slot], batch, seq)
            attn = _mla_attention(q, k, v)
            h32, h16 = _outproj_ln(attn, mla_w_out[slot], h32, g0, b0)
        h32, h16, xs_buffer = _routed_ffn_ln(h32, layer, xs_buffer, router_w[layer], router_b[layer],
                                             moe_w_gate, moe_b_gate, moe_w_up, moe_b_up, moe_w_down,
                                             moe_b_down, ln_g[layer, 1], ln_b[layer, 1])
    return h32.reshape(batch, seq, d)
```
